```python
import jax, jax.numpy as jnp
from jax import lax
import numpy as np

D_MODEL = 2048
BATCH = 1
SEQ = 16384
DEPTH = 4
DEC_BATCH = 8
DEC_SEQ = 16
PAST_LEN = 4096

CHUNK = 64
MIX_WIDTH = D_MODEL
POOL_WIDTH = MIX_WIDTH // 2
POOL_WINDOWS = (2, 4, 8, 16)
POOL_GROUPS = len(POOL_WINDOWS)
POOL_GROUP_WIDTH = POOL_WIDTH // POOL_GROUPS
POOL_HIST = max(POOL_WINDOWS) - 1
GLA_HEADS = 4
GLA_WIDTH = MIX_WIDTH - POOL_WIDTH
GLA_DV = GLA_WIDTH // GLA_HEADS
GLA_DK = GLA_DV // 2
GLA_KEY_WIDTH = GLA_HEADS * GLA_DK
GLA_RANK = 16
GLA_TAU = 16.0
D_FF = 4 * D_MODEL
N_SUB = 3
EPS = 1e-6
IN_WIDTH = POOL_WIDTH + 2 * GLA_KEY_WIDTH + 2 * GLA_WIDTH + GLA_RANK
IN_SPLITS = (POOL_WIDTH,
             POOL_WIDTH + GLA_KEY_WIDTH,
             POOL_WIDTH + 2 * GLA_KEY_WIDTH,
             POOL_WIDTH + 2 * GLA_KEY_WIDTH + GLA_WIDTH,
             POOL_WIDTH + 2 * GLA_KEY_WIDTH + 2 * GLA_WIDTH)

kernel_name = "hybrid_pool_gla_stream_step"


def _rmsnorm(x, g):
    xf = x.astype(jnp.float32)
    y = xf * lax.rsqrt(jnp.mean(xf * xf, axis=-1, keepdims=True) + EPS) * g.astype(jnp.float32)
    return y.astype(x.dtype)


def _swiglu(h, wg, wu, wd):
    return (jax.nn.silu(h @ wg) * (h @ wu)) @ wd


def _pool_mixer(u, prev, start, w_pool, pool_scale):
    L = u.shape[1]
    full = jnp.concatenate([prev.astype(u.dtype), u], axis=1)
    cs = jnp.cumsum(full.astype(jnp.float32), axis=1)
    cs = jnp.concatenate([jnp.zeros_like(cs[:, :1]), cs], axis=1)
    pos = start + jnp.arange(L)
    end = cs[:, POOL_HIST + 1:POOL_HIST + 1 + L]
    outs = []
    for gi, w in enumerate(POOL_WINDOWS):
        sl = slice(gi * POOL_GROUP_WIDTH, (gi + 1) * POOL_GROUP_WIDTH)
        wsum = end[..., sl] - cs[:, POOL_HIST + 1 - w:POOL_HIST + 1 - w + L, sl]
        cnt = jnp.minimum(pos + 1, w).astype(jnp.float32)[None, :, None]
        outs.append(wsum / cnt - u[..., sl].astype(jnp.float32))
    m = jnp.stack(outs, axis=2).astype(u.dtype)
    y = jnp.einsum('blgc,gcd->blgd', m, w_pool).reshape(u.shape) * pool_scale
    return y, full[:, -POOL_HIST:]


def _gla(q, k, v, logf, s0):
    Bn, L = q.shape[0], q.shape[1]
    pad = (-L) % CHUNK
    f32 = jnp.float32
    def blk(a):
        a = a.astype(f32)
        if pad:
            a = jnp.pad(a, ((0, 0), (0, pad), (0, 0), (0, 0)))
        n = a.shape[1] // CHUNK
        return a.reshape(Bn, n, CHUNK, GLA_HEADS, a.shape[-1]).transpose(0, 3, 1, 2, 4)
    q, k, v, logf = blk(q), blk(k), blk(v), blk(logf)
    n_blk = q.shape[2]
    b = jnp.cumsum(logf, axis=3)
    b_last = b[:, :, :, -1:, :]
    qe = q * jnp.exp(b) * (GLA_DK ** -0.5)
    ke = k * jnp.exp(-b)
    kd = k * jnp.exp(b_last - b)
    mask = jnp.tril(jnp.ones((CHUNK, CHUNK), dtype=bool))
    att = jnp.where(mask, jnp.einsum('bhnik,bhnjk->bhnij', qe, ke), 0.0)
    o = jnp.einsum('bhnij,bhnjv->bhniv', att, v)
    kv = jnp.einsum('bhnjk,bhnjv->bhnkv', kd, v)
    decay = jnp.exp(b_last[:, :, :, 0, :])
    def step(S, inp):
        d, u = inp
        return d[..., None] * S + u, S
    s_fin, s_start = lax.scan(step, s0.astype(f32), (jnp.moveaxis(decay, 2, 0), jnp.moveaxis(kv, 2, 0)))
    s_start = jnp.moveaxis(s_start, 0, 2)
    o = o + jnp.einsum('bhnik,bhnkv->bhniv', qe, s_start)
    o = o.transpose(0, 2, 3, 1, 4).reshape(Bn, n_blk * CHUNK, GLA_HEADS, GLA_DV)[:, :L]
    return o, s_fin


def _layer(x, c, pool_prev, gla_prev, start, w_ada, b_ada, norm_pre, norm_post, w_ffn_gate, w_ffn_up,
           w_ffn_down, w_in, w_forget, b_forget, w_pool, pool_scale, gla_norm, w_out):
    Bn, L, _ = x.shape
    mod = jax.nn.silu(c.astype(jnp.float32)) @ w_ada.astype(jnp.float32) + b_ada.astype(jnp.float32)
    mod = mod.astype(x.dtype).reshape(Bn, N_SUB, 3, 1, D_MODEL)

    def pre(h, i):
        return _rmsnorm(h, norm_pre[i]) * (1 + mod[:, i, 1]) + mod[:, i, 0]

    def post(h, i):
        return mod[:, i, 2] * _rmsnorm(h, norm_post[i])

    h = pre(x, 0)
    x = x + 0.5 * post(_swiglu(h, w_ffn_gate[0], w_ffn_up[0], w_ffn_down[0]), 0)

    h = pre(x, 1)
    z = h @ w_in
    u, q, k, v, og, fr = jnp.split(z, IN_SPLITS, axis=-1)
    y_pool, pool_new = _pool_mixer(u, pool_prev, start, w_pool, pool_scale)
    logf = jax.nn.log_sigmoid((fr @ w_forget + b_forget).astype(jnp.float32)) / GLA_TAU
    o, gla_new = _gla(q.reshape(Bn, L, GLA_HEADS, GLA_DK), k.reshape(Bn, L, GLA_HEADS, GLA_DK),
                      v.reshape(Bn, L, GLA_HEADS, GLA_DV), logf.reshape(Bn, L, GLA_HEADS, GLA_DK), gla_prev)
    o = _rmsnorm(o.astype(x.dtype), gla_norm.reshape(GLA_HEADS, GLA_DV)).reshape(Bn, L, GLA_WIDTH)
    o = o * jax.nn.silu(og)
    mix = jnp.concatenate([y_pool, o], axis=-1) @ w_out
    x = x + post(mix, 1)

    h = pre(x, 2)
    x = x + 0.5 * post(_swiglu(h, w_ffn_gate[1], w_ffn_up[1], w_ffn_down[1]), 2)
    return x, pool_new, gla_new


def setup_inputs(seed: int = 0) -> dict:
    key = jax.random.key(seed)
    ks = jax.random.split(key, 20)
    def nrm(k, shape, s):
        return jax.random.normal(k, shape, jnp.float32) * s
    return {
        "x_prompt": nrm(ks[0], (BATCH, SEQ, D_MODEL), 1.0),
        "x_sample": nrm(ks[1], (DEC_BATCH, DEC_SEQ, D_MODEL), 1.0),
        "state_pool": nrm(ks[2], (DEPTH, DEC_BATCH, POOL_HIST, POOL_WIDTH), 1.0),
        "state_gla": nrm(ks[3], (DEPTH, DEC_BATCH, GLA_HEADS, GLA_DK, GLA_DV), 1.0),
        "c_prompt": nrm(ks[4], (BATCH, D_MODEL), 1.0),
        "c_sample": nrm(ks[5], (DEC_BATCH, D_MODEL), 1.0),
        "w_ada": nrm(ks[6], (DEPTH, D_MODEL, N_SUB * 3 * D_MODEL), 0.5 * D_MODEL ** -0.5),
        "b_ada": nrm(ks[7], (DEPTH, N_SUB * 3 * D_MODEL), 0.02),
        "norm_pre": 1.0 + nrm(ks[8], (DEPTH, N_SUB, D_MODEL), 0.02),
        "norm_post": 1.0 + nrm(ks[9], (DEPTH, N_SUB, D_MODEL), 0.02),
        "w_ffn_gate": nrm(ks[10], (DEPTH, 2, D_MODEL, D_FF), D_MODEL ** -0.5),
        "w_ffn_up": nrm(ks[11], (DEPTH, 2, D_MODEL, D_FF), D_MODEL ** -0.5),
        "w_ffn_down": nrm(ks[12], (DEPTH, 2, D_FF, D_MODEL), D_FF ** -0.5),
        "w_in": nrm(ks[13], (DEPTH, D_MODEL, IN_WIDTH), D_MODEL ** -0.5),
        "w_forget": nrm(ks[14], (DEPTH, GLA_RANK, GLA_KEY_WIDTH), GLA_RANK ** -0.5),
        "b_forget": nrm(ks[15], (DEPTH, GLA_KEY_WIDTH), 0.1),
        "w_pool": nrm(ks[16], (DEPTH, POOL_GROUPS, POOL_GROUP_WIDTH, POOL_GROUP_WIDTH), POOL_GROUP_WIDTH ** -0.5),
        "pool_scale": 1.0 + nrm(ks[17], (DEPTH, POOL_WIDTH), 0.1),
        "gla_norm": 1.0 + nrm(ks[18], (DEPTH, GLA_WIDTH), 0.02),
        "w_out": nrm(ks[19], (DEPTH, MIX_WIDTH, D_MODEL), MIX_WIDTH ** -0.5),
    }


def reference(x_prompt, x_sample, state_pool, state_gla, c_prompt, c_sample, w_ada, b_ada, norm_pre,
              norm_post, w_ffn_gate, w_ffn_up, w_ffn_down, w_in, w_forget, b_forget, w_pool, pool_scale,
              gla_norm, w_out):
    xp, xs = x_prompt, x_sample
    bp = x_prompt.shape[0]
    pool0 = jnp.zeros((bp, POOL_HIST, POOL_WIDTH), x_prompt.dtype)
    gla0 = jnp.zeros((bp, GLA_HEADS, GLA_DK, GLA_DV), jnp.float32)
    pool_p, gla_p, pool_s, gla_s = [], [], [], []
    for l in range(DEPTH):
        lw = (w_ada[l], b_ada[l], norm_pre[l], norm_post[l], w_ffn_gate[l], w_ffn_up[l], w_ffn_down[l],
              w_in[l], w_forget[l], b_forget[l], w_pool[l], pool_scale[l], gla_norm[l], w_out[l])
        xp, pp, gp = _layer(xp, c_prompt, pool0, gla0, 0, *lw)
        xs, ps, gs = _layer(xs, c_sample, state_pool[l], state_gla[l], PAST_LEN, *lw)
        pool_p.append(pp)
        gla_p.append(gp)
        pool_s.append(ps)
        gla_s.append(gs)
    return (xp, xs, jnp.stack(pool_p), jnp.stack(gla_p), jnp.stack(pool_s), jnp.stack(gla_s))
```

```python
import functools

import jax
import jax.numpy as jnp
from jax import lax
from jax.experimental import pallas as pl
from jax.experimental.pallas import tpu as pltpu

F32 = jnp.float32
BF16 = jnp.bfloat16

EPS = 1e-6
N_SUB = 3
CHUNK = 64
CHUNK_SHIFT = CHUNK.bit_length() - 1
assert CHUNK == 1 << CHUNK_SHIFT
POOL_WINDOWS = (2, 4, 8, 16)
POOL_HIST = max(POOL_WINDOWS) - 1
HIST_ROWS = POOL_HIST + 1
GLA_HEADS = 4
GLA_TAU = 16.0
GLA_RANK_PAD = 128
PAST_LEN = 4096

ROW_CHUNK = 16
VMEM_LIMIT_BYTES = 56 * 1024 * 1024


def _silu(x):
    return x * jax.nn.sigmoid(x)


def _log_sigmoid(x):
    return jnp.minimum(x, 0.0) - jnp.log1p(jnp.exp(-jnp.abs(x)))


def _rms(x, g):
    return x * lax.rsqrt(jnp.mean(x * x, axis=-1, keepdims=True) + EPS) * g


def _dot(a, b):
    return jnp.dot(a, b, preferred_element_type=F32)


def _dot_nt(a, b):
    return lax.dot_general(a, b, (((1,), (1,)), ((), ())), preferred_element_type=F32)


def _ada_kernel(c_ref, w_ref, b_ref, o_ref):
    c = c_ref[...]
    o_ref[0] = _dot(_silu(c).astype(BF16), w_ref[0].astype(BF16)) + b_ref[0]


def _ada(c_all, w_ada, b_ada, tn=1024):
    depth, d, n = w_ada.shape
    rows = c_all.shape[0]
    return pl.pallas_call(
        _ada_kernel,
        grid=(depth, n // tn),
        in_specs=[
            pl.BlockSpec((rows, d), lambda l, j: (0, 0)),
            pl.BlockSpec((1, d, tn), lambda l, j: (l, 0, j)),
            pl.BlockSpec((1, 1, tn), lambda l, j: (l, 0, j)),
        ],
        out_specs=pl.BlockSpec((1, rows, tn), lambda l, j: (l, 0, j)),
        out_shape=jax.ShapeDtypeStruct((depth, rows, n), F32),
        compiler_params=pltpu.CompilerParams(
            dimension_semantics=("arbitrary", "arbitrary"), vmem_limit_bytes=VMEM_LIMIT_BYTES),
        name="ada",
    )(c_all, w_ada, b_ada.reshape(depth, 1, n))


def _ffn_kernel(x_ref, shift_ref, scale_ref, gate_ref, gpre_ref, gpost_ref, wg_ref, wu_ref, wd_ref,
                o_ref, h_ref, *, per_row_mod):
    j = pl.program_id(1)
    tm = x_ref.shape[0]
    n_chunks = tm // ROW_CHUNK

    def mod_rows(ref, sl):
        return ref[sl, :] if per_row_mod else ref[...]

    @pl.when(j == 0)
    def _():
        def body(r, carry):
            sl = pl.ds(pl.multiple_of(r * ROW_CHUNK, ROW_CHUNK), ROW_CHUNK)
            xn = _rms(x_ref[sl, :], gpre_ref[...])
            h = xn * (1.0 + mod_rows(scale_ref, sl)) + mod_rows(shift_ref, sl)
            h_ref[sl, :] = h.astype(BF16)
            o_ref[sl, :] = jnp.zeros((ROW_CHUNK, o_ref.shape[1]), F32)
            return carry
        lax.fori_loop(0, n_chunks, body, 0)

    h = h_ref[...]
    g = _dot(h, wg_ref[...])
    u = _dot(h, wu_ref[...])
    a = (_silu(g) * u).astype(BF16)
    o_ref[...] += _dot(a, wd_ref[...])

    @pl.when(j == pl.num_programs(1) - 1)
    def _():
        def body(r, carry):
            sl = pl.ds(pl.multiple_of(r * ROW_CHUNK, ROW_CHUNK), ROW_CHUNK)
            y = mod_rows(gate_ref, sl) * _rms(o_ref[sl, :], gpost_ref[...])
            o_ref[sl, :] = x_ref[sl, :] + 0.5 * y
            return carry
        lax.fori_loop(0, n_chunks, body, 0)


def _ffn(x, shift, scale, gate, gpre, gpost, wg, wu, wd, layer, sub, *, tm, tf):
    r, d = x.shape
    f = wg.shape[-1]
    per_row_mod = shift.shape[0] != 1
    mod_spec = (pl.BlockSpec((tm, d), lambda i, j: (i, 0)) if per_row_mod
                else pl.BlockSpec((1, d), lambda i, j: (0, 0)))
    vec_spec = pl.BlockSpec((1, d), lambda i, j: (0, 0))
    return pl.pallas_call(
        functools.partial(_ffn_kernel, per_row_mod=per_row_mod),
        grid=(r // tm, f // tf),
        in_specs=[
            pl.BlockSpec((tm, d), lambda i, j: (i, 0)),
            mod_spec, mod_spec, mod_spec, vec_spec, vec_spec,
            pl.BlockSpec((None, None, d, tf), lambda i, j: (layer, sub, 0, j)),
            pl.BlockSpec((None, None, d, tf), lambda i, j: (layer, sub, 0, j)),
            pl.BlockSpec((None, None, tf, d), lambda i, j: (layer, sub, j, 0)),
        ],
        out_specs=pl.BlockSpec((tm, d), lambda i, j: (i, 0)),
        out_shape=jax.ShapeDtypeStruct((r, d), F32),
        scratch_shapes=[pltpu.VMEM((tm, d), BF16)],
        compiler_params=pltpu.CompilerParams(
            dimension_semantics=("arbitrary", "arbitrary"), vmem_limit_bytes=VMEM_LIMIT_BYTES),
        name="ffn",
    )(x, shift, scale, gate, gpre, gpost, wg, wu, wd)


def _split3(x):
    hi = x.astype(BF16)
    r1 = x - hi.astype(F32)
    mid = r1.astype(BF16)
    lo = (r1 - mid.astype(F32)).astype(BF16)
    return hi, mid, lo


def _pad_rows(a, rows):
    if a.shape[0] == rows:
        return a
    return jnp.concatenate([a, jnp.zeros((rows - a.shape[0], a.shape[1]), a.dtype)], axis=0)


def _mixer_kernel(x_ref, shift_ref, scale_ref, gate_ref, gpre_ref, gpost_ref, win_ref, wfr_ref, wfg_ref,
                  bfg_ref, wpool_ref, pscale_ref, gnorm_ref, wout_ref, pool_prev_ref, gla_prev_ref,
                  o_ref, pool_new_ref, gla_new_ref, z_ref, ext_ref, s_ref, mix_ref, *, start, tp):
    t = pl.program_id(1)
    tm, d = x_ref.shape
    pw = ext_ref.shape[1]
    gw = pw // len(POOL_WINDOWS)
    kw = wfg_ref.shape[1]
    dk = kw // GLA_HEADS
    vw = gnorm_ref.shape[1]
    dv = vw // GLA_HEADS
    q0, k0, v0, g0 = pw, pw + kw, pw + 2 * kw, pw + 2 * kw + vw
    n_blk = tp // CHUNK

    @pl.when(t == 0)
    def _():
        ext_ref[0:HIST_ROWS, :] = pool_prev_ref[...]
        s_ref[...] = gla_prev_ref[...]

    h = (_rms(x_ref[...], gpre_ref[...]) * (1.0 + scale_ref[...]) + shift_ref[...]).astype(BF16)
    z_ref[...] = _dot(h, win_ref[...])
    fr = _dot(h, wfr_ref[...])

    ext_ref[HIST_ROWS:HIST_ROWS + tm, :] = z_ref[:, 0:pw]
    pos = start + t * tm + lax.broadcasted_iota(jnp.int32, (tm, 1), 0)
    for gi, w in enumerate(POOL_WINDOWS):
        cols = slice(gi * gw, (gi + 1) * gw)
        u = ext_ref[HIST_ROWS:HIST_ROWS + tm, cols]
        wsum = u
        for back in range(1, w):
            wsum = wsum + ext_ref[HIST_ROWS - back:HIST_ROWS - back + tm, cols]
        cnt = jnp.minimum(pos + 1, w).astype(F32)
        m = (wsum / cnt - u).astype(BF16)
        y = _dot(m, wpool_ref[gi]) * pscale_ref[:, cols]
        mix_ref[:, cols] = y.astype(BF16)
    hist = ext_ref[tm:tm + HIST_ROWS, :]
    ext_ref[0:HIST_ROWS, :] = hist
    pool_new_ref[...] = hist

    logf = _log_sigmoid(_dot(fr.astype(BF16), wfg_ref[...]) + bfg_ref[...]) / GLA_TAU
    logf = _pad_rows(logf, tp)
    if tp != tm:
        logf = jnp.where(lax.broadcasted_iota(jnp.int32, (tp, 1), 0) < tm, logf, 0.0)
    row = lax.broadcasted_iota(jnp.int32, (tp, tp), 0)
    col = lax.broadcasted_iota(jnp.int32, (tp, tp), 1)
    causal = jnp.logical_and(row >> CHUNK_SHIFT == col >> CHUNK_SHIFT, col <= row)
    tril = jnp.where(causal, 1.0, 0.0).astype(BF16)
    hi, mid, lo = _split3(logf)
    b = _dot(tril, hi) + _dot(tril, mid) + _dot(tril, lo)

    for hh in range(GLA_HEADS):
        kc = slice(hh * dk, (hh + 1) * dk)
        q = _pad_rows(z_ref[:, q0 + hh * dk:q0 + (hh + 1) * dk], tp)
        k = _pad_rows(z_ref[:, k0 + hh * dk:k0 + (hh + 1) * dk], tp)
        v = _pad_rows(z_ref[:, v0 + hh * dv:v0 + (hh + 1) * dv], tp).astype(BF16)
        bh = b[:, kc]
        qe = (q * jnp.exp(bh) * (dk ** -0.5)).astype(BF16)
        ke = (k * jnp.exp(-bh)).astype(BF16)
        att = jnp.where(causal, _dot_nt(qe, ke), 0.0).astype(BF16)
        o = _dot(att, v)
        state = s_ref[hh]
        o_blocks = []
        for c in range(n_blk):
            rows = slice(c * CHUNK, (c + 1) * CHUNK)
            o_blocks.append(_dot(qe[rows], state.astype(BF16)))
            bt = bh[rows].T
            b_last = bt[:, CHUNK - 1:CHUNK]
            kd_t = (k[rows].T * jnp.exp(b_last - bt)).astype(BF16)
            state = jnp.exp(b_last) * state + _dot(kd_t, v[rows])
        s_ref[hh] = state
        gla_new_ref[hh] = state
        o = o + jnp.concatenate(o_blocks, axis=0) if n_blk > 1 else o + o_blocks[0]
        o = _rms(o[0:tm], gnorm_ref[:, hh * dv:(hh + 1) * dv])
        og = z_ref[:, g0 + hh * dv:g0 + (hh + 1) * dv]
        mix_ref[:, pw + hh * dv:pw + (hh + 1) * dv] = (o * _silu(og)).astype(BF16)

    y = _dot(mix_ref[...], wout_ref[...])
    o_ref[...] = x_ref[...] + gate_ref[...] * _rms(y, gpost_ref[...])


def _mixer(x, shift, scale, gate, gpre, gpost, w_in, w_fr, w_fg, b_fg, w_pool, pscale, gnorm, w_out,
           pool_prev, gla_prev, layer, *, n_seq, tm, start):
    r, d = x.shape
    n_tiles = r // n_seq // tm
    tp = max(tm, CHUNK)
    zw = w_in.shape[-1]
    pw = pool_prev.shape[-1]
    hds, dk, dv = gla_prev.shape[1:]
    const = dict(pipeline_mode=pl.Buffered(1))

    def seq_spec(shape):
        return pl.BlockSpec((None,) + shape, lambda s, t: (s,) + (0,) * len(shape))

    def layer_spec(shape):
        return pl.BlockSpec((None,) + shape, lambda s, t: (layer,) + (0,) * len(shape), **const)

    row_spec = pl.BlockSpec((tm, d), lambda s, t: (s * n_tiles + t, 0))
    return pl.pallas_call(
        functools.partial(_mixer_kernel, start=start, tp=tp),
        grid=(n_seq, n_tiles),
        in_specs=[
            row_spec,
            seq_spec((1, d)), seq_spec((1, d)), seq_spec((1, d)),
            pl.BlockSpec((1, d), lambda s, t: (0, 0)), pl.BlockSpec((1, d), lambda s, t: (0, 0)),
            layer_spec((d, zw)), layer_spec((d, GLA_RANK_PAD)), layer_spec((GLA_RANK_PAD, hds * dk)),
            layer_spec((1, hds * dk)), layer_spec(w_pool.shape[1:]), layer_spec((1, pw)),
            layer_spec((1, hds * dv)), layer_spec((pw + hds * dv, d)),
            seq_spec((HIST_ROWS, pw)), seq_spec((hds, dk, dv)),
        ],
        out_specs=[row_spec, seq_spec((HIST_ROWS, pw)), seq_spec((hds, dk, dv))],
        out_shape=[
            jax.ShapeDtypeStruct((r, d), F32),
            jax.ShapeDtypeStruct((n_seq, HIST_ROWS, pw), F32),
            jax.ShapeDtypeStruct((n_seq, hds, dk, dv), F32),
        ],
        scratch_shapes=[
            pltpu.VMEM((tm, zw), F32),
            pltpu.VMEM((HIST_ROWS + tm, pw), F32),
            pltpu.VMEM((hds, dk, dv), F32),
            pltpu.VMEM((tm, pw + hds * dv), BF16),
        ],
        compiler_params=pltpu.CompilerParams(
            dimension_semantics=("arbitrary", "arbitrary"), vmem_limit_bytes=VMEM_LIMIT_BYTES),
        name="mixer",
    )(x, shift, scale, gate, gpre, gpost, w_in, w_fr, w_fg, b_fg, w_pool, pscale, gnorm, w_out,
      pool_prev, gla_prev)


def _tiles(rows, d_ff):
    tm = min(rows, 512)
    tf = min(d_ff, 512)
    return tm, tf


def kernel(x_prompt, x_sample, state_pool, state_gla, c_prompt, c_sample, w_ada, b_ada, norm_pre, norm_post,
           w_ffn_gate, w_ffn_up, w_ffn_down, w_in, w_forget, b_forget, w_pool, pool_scale, gla_norm, w_out):
    bp, seq, d = x_prompt.shape
    bs, dec_seq, _ = x_sample.shape
    depth = w_ada.shape[0]
    pw = state_pool.shape[-1]
    hds, dk, dv = state_gla.shape[2:]
    kw, vw = hds * dk, hds * dv
    zw = pw + 2 * kw + 2 * vw
    rank = w_forget.shape[1]
    d_ff = w_ffn_gate.shape[-1]

    n_c = bp + bs
    c_rows = -(-n_c // 16) * 16
    c_all = jnp.concatenate([c_prompt, c_sample, jnp.zeros((c_rows - n_c, d), F32)], axis=0)
    mod = _ada(c_all, w_ada, b_ada).reshape(depth, c_rows, N_SUB, 3, d)

    wg, wu, wd = w_ffn_gate.astype(BF16), w_ffn_up.astype(BF16), w_ffn_down.astype(BF16)
    w_in_main = w_in[:, :, :zw].astype(BF16)
    w_fr = jnp.pad(w_in[:, :, zw:], ((0, 0), (0, 0), (0, GLA_RANK_PAD - rank))).astype(BF16)
    w_fg = jnp.pad(w_forget, ((0, 0), (0, GLA_RANK_PAD - rank), (0, 0))).astype(BF16)
    w_pool_b, w_out_b = w_pool.astype(BF16), w_out.astype(BF16)

    xp = x_prompt.reshape(bp * seq, d)
    xs = x_sample.reshape(bs * dec_seq, d)
    pool0 = jnp.zeros((bp, HIST_ROWS, pw), F32)
    gla0 = jnp.zeros((bp, hds, dk, dv), F32)
    pool_s_in = jnp.pad(state_pool, ((0, 0), (0, 0), (1, 0), (0, 0)))
    tm_p, tf_p = _tiles(bp * seq, d_ff)
    tm_s, tf_s = _tiles(bs * dec_seq, d_ff)
    tm_mix = min(seq, 256)

    outs = ([], [], [], [])
    for l in range(depth):
        gpre, gpost = norm_pre[l][:, None, :], norm_post[l][:, None, :]
        mp = mod[l, :bp]
        ms = mod[l, bp:n_c]
        ms_rows = jnp.repeat(ms, dec_seq, axis=0)
        lw = (w_in_main, w_fr, w_fg, b_forget[:, None, :], w_pool_b, pool_scale[:, None, :],
              gla_norm[:, None, :], w_out_b)

        def ffn_p(x, sub, i):
            return _ffn(x, mp[:, i, 0], mp[:, i, 1], mp[:, i, 2], gpre[i], gpost[i], wg, wu, wd, l, sub,
                        tm=tm_p, tf=tf_p)

        def ffn_s(x, sub, i):
            return _ffn(x, ms_rows[:, i, 0], ms_rows[:, i, 1], ms_rows[:, i, 2], gpre[i], gpost[i], wg, wu, wd,
                        l, sub, tm=tm_s, tf=tf_s)

        xp = ffn_p(xp, 0, 0)
        xs = ffn_s(xs, 0, 0)
        xp, pool_p, gla_p = _mixer(xp, mp[:, 1, 0][:, None], mp[:, 1, 1][:, None], mp[:, 1, 2][:, None],
                                   gpre[1], gpost[1], *lw, pool0, gla0, l, n_seq=bp, tm=tm_mix, start=0)
        xs, pool_s, gla_s = _mixer(xs, ms[:, 1, 0][:, None], ms[:, 1, 1][:, None], ms[:, 1, 2][:, None],
                                   gpre[1], gpost[1], *lw, pool_s_in[l], state_gla[l], l, n_seq=bs, tm=dec_seq,
                                   start=PAST_LEN)
        xp = ffn_p(xp, 1, 2)
        xs = ffn_s(xs, 1, 2)
        for acc, val in zip(outs, (pool_p[:, 1:], gla_p, pool_s[:, 1:], gla_s)):
            acc.append(val)

    return (xp.reshape(bp, seq, d), xs.reshape(bs, dec_seq, d),
            jnp.stack(outs[0]), jnp.stack(outs[1]), jnp.stack(outs[2]), jnp.stack(outs[3]))
```

```python
import functools

import jax
import jax.numpy as jnp
from jax import lax
from jax.experimental import pallas as pl
from jax.experimental.pallas import tpu as pltpu

F32 = jnp.float32
BF16 = jnp.bfloat16

EPS = 1e-6
N_SUB = 3
CHUNK = 64
CHUNK_SHIFT = CHUNK.bit_length() - 1
assert CHUNK == 1 << CHUNK_SHIFT
POOL_WINDOWS = (2, 4, 8, 16)
POOL_HIST = max(POOL_WINDOWS) - 1
HIST_ROWS = POOL_HIST + 1
GLA_HEADS = 4
GLA_TAU = 16.0
GLA_RANK_PAD = 128
PAST_LEN = 4096

ROW_CHUNK = 16
VMEM_LIMIT_BYTES = 56 * 1024 * 1024


def _silu(x):
    return x * jax.nn.sigmoid(x)


def _log_sigmoid(x):
    return jnp.minimum(x, 0.0) - jnp.log1p(jnp.exp(-jnp.abs(x)))


def _rms(x, g):
    return x * lax.rsqrt(jnp.mean(x * x, axis=-1, keepdims=True) + EPS) * g


def _dot(a, b):
    return jnp.dot(a, b, preferred_element_type=F32)


def _dot_nt(a, b):
    return lax.dot_general(a, b, (((1,), (1,)), ((), ())), preferred_element_type=F32)


def _ada_kernel(c_ref, w_ref, b_ref, o_ref):
    c = c_ref[...]
    o_ref[0] = _dot(_silu(c).astype(BF16), w_ref[0].astype(BF16)) + b_ref[0]


def _ada(c_all, w_ada, b_ada, tn=1024):
    depth, d, n = w_ada.shape
    rows = c_all.shape[0]
    return pl.pallas_call(
        _ada_kernel,
        grid=(depth, n // tn),
        in_specs=[
            pl.BlockSpec((rows, d), lambda l, j: (0, 0)),
            pl.BlockSpec((1, d, tn), lambda l, j: (l, 0, j)),
            pl.BlockSpec((1, 1, tn), lambda l, j: (l, 0, j)),
        ],
        out_specs=pl.BlockSpec((1, rows, tn), lambda l, j: (l, 0, j)),
        out_shape=jax.ShapeDtypeStruct((depth, rows, n), F32),
        compiler_params=pltpu.CompilerParams(
            dimension_semantics=("arbitrary", "arbitrary"), vmem_limit_bytes=VMEM_LIMIT_BYTES),
        name="ada",
    )(c_all, w_ada, b_ada.reshape(depth, 1, n))


def _swiglu_accumulate(h, wg, wu, wd):
    return _dot((_silu(_dot(h, wg)) * _dot(h, wu)).astype(BF16), wd)


def _ffn_rows_kernel(x_ref, shift_ref, scale_ref, gate_ref, gpre_ref, gpost_ref, wg_ref, wu_ref, wd_ref,
                     o_ref, wgb_ref, wub_ref, wdb_ref, h_ref):
    j = pl.program_id(0)
    n_chunks = x_ref.shape[0] // ROW_CHUNK

    @pl.when(j == 0)
    def _():
        def body(r, carry):
            sl = pl.ds(pl.multiple_of(r * ROW_CHUNK, ROW_CHUNK), ROW_CHUNK)
            xn = _rms(x_ref[sl, :], gpre_ref[...])
            h_ref[sl, :] = (xn * (1.0 + scale_ref[sl, :]) + shift_ref[sl, :]).astype(BF16)
            o_ref[sl, :] = jnp.zeros((ROW_CHUNK, o_ref.shape[1]), F32)
            return carry
        lax.fori_loop(0, n_chunks, body, 0)

    wg, wu, wd = wg_ref[...].astype(BF16), wu_ref[...].astype(BF16), wd_ref[...].astype(BF16)
    wgb_ref[...] = wg
    wub_ref[...] = wu
    wdb_ref[...] = wd
    o_ref[...] += _swiglu_accumulate(h_ref[...], wg, wu, wd)

    @pl.when(j == pl.num_programs(0) - 1)
    def _():
        def body(r, carry):
            sl = pl.ds(pl.multiple_of(r * ROW_CHUNK, ROW_CHUNK), ROW_CHUNK)
            y = gate_ref[sl, :] * _rms(o_ref[sl, :], gpost_ref[...])
            o_ref[sl, :] = x_ref[sl, :] + 0.5 * y
            return carry
        lax.fori_loop(0, n_chunks, body, 0)


def _ffn_rows(x, shift, scale, gate, gpre, gpost, wg, wu, wd, layer, sub, *, tf):
    r, d = x.shape
    f = wg.shape[-1]
    row_spec = pl.BlockSpec((r, d), lambda j: (0, 0))
    vec_spec = pl.BlockSpec((1, d), lambda j: (0, 0))
    return pl.pallas_call(
        _ffn_rows_kernel,
        grid=(f // tf,),
        in_specs=[
            row_spec, row_spec, row_spec, row_spec, vec_spec, vec_spec,
            pl.BlockSpec((None, None, d, tf), lambda j: (layer, sub, 0, j)),
            pl.BlockSpec((None, None, d, tf), lambda j: (layer, sub, 0, j)),
            pl.BlockSpec((None, None, tf, d), lambda j: (layer, sub, j, 0)),
        ],
        out_specs=[
            row_spec,
            pl.BlockSpec((d, tf), lambda j: (0, j)),
            pl.BlockSpec((d, tf), lambda j: (0, j)),
            pl.BlockSpec((tf, d), lambda j: (j, 0)),
        ],
        out_shape=[
            jax.ShapeDtypeStruct((r, d), F32),
            jax.ShapeDtypeStruct((d, f), BF16),
            jax.ShapeDtypeStruct((d, f), BF16),
            jax.ShapeDtypeStruct((f, d), BF16),
        ],
        scratch_shapes=[pltpu.VMEM((r, d), BF16)],
        compiler_params=pltpu.CompilerParams(
            dimension_semantics=("arbitrary",), vmem_limit_bytes=VMEM_LIMIT_BYTES),
        name="ffn_rows",
    )(x, shift, scale, gate, gpre, gpost, wg, wu, wd)


def _ffn_stream_kernel(xa_ref, xc_ref, shift_ref, scale_ref, gate_ref, gpre_ref, gpost_ref, wg_ref, wu_ref,
                       wd_ref, o_ref, h0_ref, h1_ref, acc0_ref, acc1_ref, *, n_tiles):
    i = pl.program_id(0)
    j = pl.program_id(1)
    rc = xa_ref.shape[0]
    row0 = pl.multiple_of(j * rc, rc)
    zeros = jnp.zeros((ROW_CHUNK, acc0_ref.shape[1]), F32)

    def pre_norm(h_ref, clear_refs):
        for s in range(rc // ROW_CHUNK):
            rows = pl.ds(row0 + s * ROW_CHUNK, ROW_CHUNK)
            xn = _rms(xa_ref[s * ROW_CHUNK:(s + 1) * ROW_CHUNK, :], gpre_ref[...])
            h_ref[rows, :] = (xn * (1.0 + scale_ref[...]) + shift_ref[...]).astype(BF16)
            for acc_ref in clear_refs:
                acc_ref[rows, :] = zeros

    def post_norm(acc_ref):
        for s in range(rc // ROW_CHUNK):
            sl = slice(s * ROW_CHUNK, (s + 1) * ROW_CHUNK)
            y = gate_ref[...] * _rms(acc_ref[pl.ds(row0 + s * ROW_CHUNK, ROW_CHUNK), :], gpost_ref[...])
            o_ref[sl, :] = xc_ref[sl, :] + 0.5 * y

    @pl.when(i == 0)
    def _():
        pre_norm(h0_ref, (acc0_ref, acc1_ref))

    for parity, (h_new, acc_new, h_mm, acc_mm) in enumerate(
            ((h0_ref, acc0_ref, h1_ref, acc1_ref), (h1_ref, acc1_ref, h0_ref, acc0_ref))):
        is_mine = lax.rem(i, 2) == parity

        @pl.when(jnp.logical_and(is_mine, jnp.logical_and(i >= 1, i <= n_tiles)))
        def _():
            post_norm(acc_new)
            pre_norm(h_new, (acc_new,))
            acc_mm[...] += _swiglu_accumulate(h_mm[...], wg_ref[...], wu_ref[...], wd_ref[...])

        @pl.when(jnp.logical_and(is_mine, i == n_tiles + 1))
        def _():
            post_norm(acc_new)


def _ffn_stream(x, shift, scale, gate, gpre, gpost, wg, wu, wd, *, tm, tf):
    r, d = x.shape
    f = wg.shape[-1]
    assert shift.shape == (1, d)
    nt, nj = r // tm, f // tf
    rc = tm // nj
    assert rc % ROW_CHUNK == 0 and nt >= 2

    def w_block(i, j):
        return jnp.where(i == 0, 0, jnp.where(i == nt + 1, nj - 1, j))

    vec_spec = pl.BlockSpec((1, d), lambda i, j: (0, 0))
    return pl.pallas_call(
        functools.partial(_ffn_stream_kernel, n_tiles=nt),
        grid=(nt + 2, nj),
        in_specs=[
            pl.BlockSpec((rc, d), lambda i, j: (jnp.minimum(i, nt - 1) * nj + j, 0)),
            pl.BlockSpec((rc, d), lambda i, j: (jnp.maximum(i - 2, 0) * nj + j, 0)),
            vec_spec, vec_spec, vec_spec, vec_spec, vec_spec,
            pl.BlockSpec((d, tf), lambda i, j: (0, w_block(i, j))),
            pl.BlockSpec((d, tf), lambda i, j: (0, w_block(i, j))),
            pl.BlockSpec((tf, d), lambda i, j: (w_block(i, j), 0)),
        ],
        out_specs=pl.BlockSpec((rc, d), lambda i, j: (jnp.maximum(i - 2, 0) * nj + jnp.where(i >= 2, j, 0), 0)),
        out_shape=jax.ShapeDtypeStruct((r, d), F32),
        scratch_shapes=[pltpu.VMEM((tm, d), BF16), pltpu.VMEM((tm, d), BF16),
                        pltpu.VMEM((tm, d), F32), pltpu.VMEM((tm, d), F32)],
        compiler_params=pltpu.CompilerParams(
            dimension_semantics=("arbitrary", "arbitrary"), vmem_limit_bytes=VMEM_LIMIT_BYTES),
        name="ffn_stream",
    )(x, x, shift, scale, gate, gpre, gpost, wg, wu, wd)


def _split3(x):
    hi = x.astype(BF16)
    r1 = x - hi.astype(F32)
    mid = r1.astype(BF16)
    lo = (r1 - mid.astype(F32)).astype(BF16)
    return hi, mid, lo


def _pad_rows(a, rows):
    if a.shape[0] == rows:
        return a
    return jnp.concatenate([a, jnp.zeros((rows - a.shape[0], a.shape[1]), a.dtype)], axis=0)


def _mixer_kernel(x_ref, shift_ref, scale_ref, gate_ref, gpre_ref, gpost_ref, win_ref, wfr_ref, wfg_ref,
                  bfg_ref, wpool_ref, pscale_ref, gnorm_ref, wout_ref, pool_prev_ref, gla_prev_ref,
                  o_ref, pool_new_ref, gla_new_ref, z_ref, ext_ref, s_ref, mix_ref, *, start, tp):
    t = pl.program_id(1)
    tm, d = x_ref.shape
    pw = ext_ref.shape[1]
    gw = pw // len(POOL_WINDOWS)
    kw = wfg_ref.shape[1]
    dk = kw // GLA_HEADS
    vw = gnorm_ref.shape[1]
    dv = vw // GLA_HEADS
    q0, k0, v0, g0 = pw, pw + kw, pw + 2 * kw, pw + 2 * kw + vw
    n_blk = tp // CHUNK

    @pl.when(t == 0)
    def _():
        ext_ref[0:HIST_ROWS, :] = pool_prev_ref[...]
        s_ref[...] = gla_prev_ref[...]

    h = (_rms(x_ref[...], gpre_ref[...]) * (1.0 + scale_ref[...]) + shift_ref[...]).astype(BF16)
    z_ref[...] = _dot(h, win_ref[...])
    fr = _dot(h, wfr_ref[...])

    ext_ref[HIST_ROWS:HIST_ROWS + tm, :] = z_ref[:, 0:pw]
    pos = start + t * tm + lax.broadcasted_iota(jnp.int32, (tm, 1), 0)
    for gi, w in enumerate(POOL_WINDOWS):
        cols = slice(gi * gw, (gi + 1) * gw)
        u = ext_ref[HIST_ROWS:HIST_ROWS + tm, cols]
        wsum = u
        for back in range(1, w):
            wsum = wsum + ext_ref[HIST_ROWS - back:HIST_ROWS - back + tm, cols]
        cnt = jnp.minimum(pos + 1, w).astype(F32)
        m = (wsum / cnt - u).astype(BF16)
        y = _dot(m, wpool_ref[gi]) * pscale_ref[:, cols]
        mix_ref[:, cols] = y.astype(BF16)
    hist = ext_ref[tm:tm + HIST_ROWS, :]
    ext_ref[0:HIST_ROWS, :] = hist
    pool_new_ref[...] = hist

    logf = _log_sigmoid(_dot(fr.astype(BF16), wfg_ref[...]) + bfg_ref[...]) / GLA_TAU
    logf = _pad_rows(logf, tp)
    if tp != tm:
        logf = jnp.where(lax.broadcasted_iota(jnp.int32, (tp, 1), 0) < tm, logf, 0.0)
    row = lax.broadcasted_iota(jnp.int32, (tp, tp), 0)
    col = lax.broadcasted_iota(jnp.int32, (tp, tp), 1)
    causal = jnp.logical_and(row >> CHUNK_SHIFT == col >> CHUNK_SHIFT, col <= row)
    tril = jnp.where(causal, 1.0, 0.0).astype(BF16)
    hi, mid, lo = _split3(logf)
    b = _dot(tril, hi) + _dot(tril, mid) + _dot(tril, lo)

    for hh in range(GLA_HEADS):
        kc = slice(hh * dk, (hh + 1) * dk)
        q = _pad_rows(z_ref[:, q0 + hh * dk:q0 + (hh + 1) * dk], tp)
        k = _pad_rows(z_ref[:, k0 + hh * dk:k0 + (hh + 1) * dk], tp)
        v = _pad_rows(z_ref[:, v0 + hh * dv:v0 + (hh + 1) * dv], tp).astype(BF16)
        bh = b[:, kc]
        qe = (q * jnp.exp(bh) * (dk ** -0.5)).astype(BF16)
        ke = (k * jnp.exp(-bh)).astype(BF16)
        att = jnp.where(causal, _dot_nt(qe, ke), 0.0).astype(BF16)
        o = _dot(att, v)
        state = s_ref[hh]
        o_blocks = []
        for c in range(n_blk):
            rows = slice(c * CHUNK, (c + 1) * CHUNK)
            o_blocks.append(_dot(qe[rows], state.astype(BF16)))
            bt = bh[rows].T
            b_last = bt[:, CHUNK - 1:CHUNK]
            kd_t = (k[rows].T * jnp.exp(b_last - bt)).astype(BF16)
            state = jnp.exp(b_last) * state + _dot(kd_t, v[rows])
        s_ref[hh] = state
        gla_new_ref[hh] = state
        o = o + jnp.concatenate(o_blocks, axis=0) if n_blk > 1 else o + o_blocks[0]
        o = _rms(o[0:tm], gnorm_ref[:, hh * dv:(hh + 1) * dv])
        og = z_ref[:, g0 + hh * dv:g0 + (hh + 1) * dv]
        mix_ref[:, pw + hh * dv:pw + (hh + 1) * dv] = (o * _silu(og)).astype(BF16)

    y = _dot(mix_ref[...], wout_ref[...])
    o_ref[...] = x_ref[...] + gate_ref[...] * _rms(y, gpost_ref[...])


def _mixer(x, shift, scale, gate, gpre, gpost, w_in, w_fr, w_fg, b_fg, w_pool, pscale, gnorm, w_out,
           pool_prev, gla_prev, layer, *, n_seq, tm, start):
    r, d = x.shape
    n_tiles = r // n_seq // tm
    tp = max(tm, CHUNK)
    zw = w_in.shape[-1]
    pw = pool_prev.shape[-1]
    hds, dk, dv = gla_prev.shape[1:]
    const = dict(pipeline_mode=pl.Buffered(1))

    def seq_spec(shape):
        return pl.BlockSpec((None,) + shape, lambda s, t: (s,) + (0,) * len(shape))

    def layer_spec(shape):
        return pl.BlockSpec((None,) + shape, lambda s, t: (layer,) + (0,) * len(shape), **const)

    row_spec = pl.BlockSpec((tm, d), lambda s, t: (s * n_tiles + t, 0))
    return pl.pallas_call(
        functools.partial(_mixer_kernel, start=start, tp=tp),
        grid=(n_seq, n_tiles),
        in_specs=[
            row_spec,
            seq_spec((1, d)), seq_spec((1, d)), seq_spec((1, d)),
            pl.BlockSpec((1, d), lambda s, t: (0, 0)), pl.BlockSpec((1, d), lambda s, t: (0, 0)),
            layer_spec((d, zw)), layer_spec((d, GLA_RANK_PAD)), layer_spec((GLA_RANK_PAD, hds * dk)),
            layer_spec((1, hds * dk)), layer_spec(w_pool.shape[1:]), layer_spec((1, pw)),
            layer_spec((1, hds * dv)), layer_spec((pw + hds * dv, d)),
            seq_spec((HIST_ROWS, pw)), seq_spec((hds, dk, dv)),
        ],
        out_specs=[row_spec, seq_spec((HIST_ROWS, pw)), seq_spec((hds, dk, dv))],
        out_shape=[
            jax.ShapeDtypeStruct((r, d), F32),
            jax.ShapeDtypeStruct((n_seq, HIST_ROWS, pw), F32),
            jax.ShapeDtypeStruct((n_seq, hds, dk, dv), F32),
        ],
        scratch_shapes=[
            pltpu.VMEM((tm, zw), F32),
            pltpu.VMEM((HIST_ROWS + tm, pw), F32),
            pltpu.VMEM((hds, dk, dv), F32),
            pltpu.VMEM((tm, pw + hds * dv), BF16),
        ],
        compiler_params=pltpu.CompilerParams(
            dimension_semantics=("arbitrary", "arbitrary"), vmem_limit_bytes=VMEM_LIMIT_BYTES),
        name="mixer",
    )(x, shift, scale, gate, gpre, gpost, w_in, w_fr, w_fg, b_fg, w_pool, pscale, gnorm, w_out,
      pool_prev, gla_prev)


PROMPT_ROW_TILE = 512
PROMPT_FF_TILE = 1024
SAMPLE_FF_TILE = 512
MIXER_ROW_TILE = 256


def kernel(x_prompt, x_sample, state_pool, state_gla, c_prompt, c_sample, w_ada, b_ada, norm_pre, norm_post,
           w_ffn_gate, w_ffn_up, w_ffn_down, w_in, w_forget, b_forget, w_pool, pool_scale, gla_norm, w_out):
    bp, seq, d = x_prompt.shape
    bs, dec_seq, _ = x_sample.shape
    depth = w_ada.shape[0]
    pw = state_pool.shape[-1]
    hds, dk, dv = state_gla.shape[2:]
    kw, vw = hds * dk, hds * dv
    zw = pw + 2 * kw + 2 * vw
    rank = w_forget.shape[1]
    d_ff = w_ffn_gate.shape[-1]

    n_c = bp + bs
    c_rows = -(-n_c // 16) * 16
    c_all = jnp.concatenate([c_prompt, c_sample, jnp.zeros((c_rows - n_c, d), F32)], axis=0)
    mod = _ada(c_all, w_ada, b_ada).reshape(depth, c_rows, N_SUB, 3, d)

    w_in_main = w_in[:, :, :zw].astype(BF16)
    w_fr = jnp.pad(w_in[:, :, zw:], ((0, 0), (0, 0), (0, GLA_RANK_PAD - rank))).astype(BF16)
    w_fg = jnp.pad(w_forget, ((0, 0), (0, GLA_RANK_PAD - rank), (0, 0))).astype(BF16)
    w_pool_b, w_out_b = w_pool.astype(BF16), w_out.astype(BF16)

    xp = x_prompt.reshape(bp * seq, d)
    xs = x_sample.reshape(bs * dec_seq, d)
    pool0 = jnp.zeros((bp, HIST_ROWS, pw), F32)
    gla0 = jnp.zeros((bp, hds, dk, dv), F32)
    pool_s_in = jnp.pad(state_pool, ((0, 0), (0, 0), (1, 0), (0, 0)))
    tm_p = min(PROMPT_ROW_TILE, bp * seq // 2)
    tf_p, tf_s = min(PROMPT_FF_TILE, d_ff), min(SAMPLE_FF_TILE, d_ff)
    tm_mix = min(seq, MIXER_ROW_TILE)

    outs = ([], [], [], [])
    for l in range(depth):
        gpre, gpost = norm_pre[l][:, None, :], norm_post[l][:, None, :]
        mp = mod[l, :bp]
        ms = mod[l, bp:n_c]
        ms_rows = jnp.repeat(ms, dec_seq, axis=0)
        lw = (w_in_main, w_fr, w_fg, b_forget[:, None, :], w_pool_b, pool_scale[:, None, :],
              gla_norm[:, None, :], w_out_b)

        def ffn(xp, xs, sub, i):
            xs, wg, wu, wd = _ffn_rows(xs, ms_rows[:, i, 0], ms_rows[:, i, 1], ms_rows[:, i, 2], gpre[i], gpost[i],
                                       w_ffn_gate, w_ffn_up, w_ffn_down, l, sub, tf=tf_s)
            xp = _ffn_stream(xp, mp[:, i, 0], mp[:, i, 1], mp[:, i, 2], gpre[i], gpost[i], wg, wu, wd,
                             tm=tm_p, tf=tf_p)
            return xp, xs

        xp, xs = ffn(xp, xs, 0, 0)
        xp, pool_p, gla_p = _mixer(xp, mp[:, 1, 0][:, None], mp[:, 1, 1][:, None], mp[:, 1, 2][:, None],
                                   gpre[1], gpost[1], *lw, pool0, gla0, l, n_seq=bp, tm=tm_mix, start=0)
        xs, pool_s, gla_s = _mixer(xs, ms[:, 1, 0][:, None], ms[:, 1, 1][:, None], ms[:, 1, 2][:, None],
                                   gpre[1], gpost[1], *lw, pool_s_in[l], state_gla[l], l, n_seq=bs, tm=dec_seq,
                                   start=PAST_LEN)
        xp, xs = ffn(xp, xs, 1, 2)
        for acc, val in zip(outs, (pool_p[:, 1:], gla_p, pool_s[:, 1:], gla_s)):
            acc.append(val)

    return (xp.reshape(bp, seq, d), xs.reshape(bs, dec_seq, d),
            jnp.stack(outs[0]), jnp.stack(outs[1]), jnp.stack(outs[2]), jnp.stack(outs[3]))
```

```python
import functools

import jax
import jax.numpy as jnp
from jax import lax
from jax.experimental import pallas as pl
from jax.experimental.pallas import tpu as pltpu

F32 = jnp.float32
BF16 = jnp.bfloat16

EPS = 1e-6
N_SUB = 3
CHUNK = 64
CHUNK_SHIFT = CHUNK.bit_length() - 1
assert CHUNK == 1 << CHUNK_SHIFT
POOL_WINDOWS = (2, 4, 8, 16)
POOL_HIST = max(POOL_WINDOWS) - 1
HIST_ROWS = POOL_HIST + 1
GLA_HEADS = 4
GLA_TAU = 16.0
GLA_RANK_PAD = 128
PAST_LEN = 4096

ROW_CHUNK = 16
BF16_SUBLANES = 16
VMEM_LIMIT_BYTES = 56 * 1024 * 1024


def _silu(x):
    return x * jax.nn.sigmoid(x)


def _log_sigmoid(x):
    return jnp.minimum(x, 0.0) - jnp.log1p(jnp.exp(-jnp.abs(x)))


def _rms(x, g):
    return x * lax.rsqrt(jnp.mean(x * x, axis=-1, keepdims=True) + EPS) * g


def _dot(a, b):
    return jnp.dot(a, b, preferred_element_type=F32)


def _dot_nt(a, b):
    return lax.dot_general(a, b, (((1,), (1,)), ((), ())), preferred_element_type=F32)


def _ada_kernel(c_ref, w_ref, b_ref, o_ref):
    c = c_ref[...]
    o_ref[0] = _dot(_silu(c).astype(BF16), w_ref[0].astype(BF16)) + b_ref[0]


def _ada(c_all, w_ada, b_ada, tn=1024):
    depth, d, n = w_ada.shape
    rows = c_all.shape[0]
    return pl.pallas_call(
        _ada_kernel,
        grid=(depth, n // tn),
        in_specs=[
            pl.BlockSpec((rows, d), lambda l, j: (0, 0)),
            pl.BlockSpec((1, d, tn), lambda l, j: (l, 0, j)),
            pl.BlockSpec((1, 1, tn), lambda l, j: (l, 0, j)),
        ],
        out_specs=pl.BlockSpec((1, rows, tn), lambda l, j: (l, 0, j)),
        out_shape=jax.ShapeDtypeStruct((depth, rows, n), F32),
        compiler_params=pltpu.CompilerParams(
            dimension_semantics=("arbitrary", "arbitrary"), vmem_limit_bytes=VMEM_LIMIT_BYTES),
        name="ada",
    )(c_all, w_ada, b_ada.reshape(depth, 1, n))


def _swiglu_accumulate(h, wg, wu, wd):
    return _dot((_silu(_dot(h, wg)) * _dot(h, wu)).astype(BF16), wd)


def _ffn_rows_kernel(x_ref, shift_ref, scale_ref, gate_ref, gpre_ref, gpost_ref, wg_ref, wu_ref, wd_ref,
                     o_ref, *rest):
    h_ref = rest[-1]
    j = pl.program_id(0)
    n_chunks = x_ref.shape[0] // ROW_CHUNK

    @pl.when(j == 0)
    def _():
        def body(r, carry):
            sl = pl.ds(pl.multiple_of(r * ROW_CHUNK, ROW_CHUNK), ROW_CHUNK)
            xn = _rms(x_ref[sl, :], gpre_ref[...])
            h_ref[sl, :] = (xn * (1.0 + scale_ref[sl, :]) + shift_ref[sl, :]).astype(BF16)
            o_ref[sl, :] = jnp.zeros((ROW_CHUNK, o_ref.shape[1]), F32)
            return carry
        lax.fori_loop(0, n_chunks, body, 0)

    wg, wu, wd = wg_ref[...].astype(BF16), wu_ref[...].astype(BF16), wd_ref[...].astype(BF16)
    for copy_ref, w in zip(rest[:-1], (wg, wu, wd)):
        copy_ref[...] = w
    o_ref[...] += _swiglu_accumulate(h_ref[...], wg, wu, wd)

    @pl.when(j == pl.num_programs(0) - 1)
    def _():
        def body(r, carry):
            sl = pl.ds(pl.multiple_of(r * ROW_CHUNK, ROW_CHUNK), ROW_CHUNK)
            y = gate_ref[sl, :] * _rms(o_ref[sl, :], gpost_ref[...])
            o_ref[sl, :] = x_ref[sl, :] + 0.5 * y
            return carry
        lax.fori_loop(0, n_chunks, body, 0)


def _ffn_rows(x, shift, scale, gate, gpre, gpost, wg, wu, wd, *, tf, w_index=None):
    r, d = x.shape
    f = wg.shape[-1]
    row_spec = pl.BlockSpec((r, d), lambda j: (0, 0))
    vec_spec = pl.BlockSpec((1, d), lambda j: (0, 0))
    w_copy_specs = [pl.BlockSpec((d, tf), lambda j: (0, j)), pl.BlockSpec((d, tf), lambda j: (0, j)),
                    pl.BlockSpec((tf, d), lambda j: (j, 0))]
    if w_index is None:
        w_specs, copy_specs, copy_shapes = w_copy_specs, [], []
    else:
        layer, sub = w_index
        w_specs = [pl.BlockSpec((None, None, d, tf), lambda j: (layer, sub, 0, j)),
                   pl.BlockSpec((None, None, d, tf), lambda j: (layer, sub, 0, j)),
                   pl.BlockSpec((None, None, tf, d), lambda j: (layer, sub, j, 0))]
        copy_specs = w_copy_specs
        copy_shapes = [jax.ShapeDtypeStruct((d, f), BF16), jax.ShapeDtypeStruct((d, f), BF16),
                       jax.ShapeDtypeStruct((f, d), BF16)]
    out = pl.pallas_call(
        _ffn_rows_kernel,
        grid=(f // tf,),
        in_specs=[row_spec, row_spec, row_spec, row_spec, vec_spec, vec_spec] + w_specs,
        out_specs=[row_spec] + copy_specs,
        out_shape=[jax.ShapeDtypeStruct((r, d), F32)] + copy_shapes,
        scratch_shapes=[pltpu.VMEM((r, d), BF16)],
        compiler_params=pltpu.CompilerParams(
            dimension_semantics=("arbitrary",), vmem_limit_bytes=VMEM_LIMIT_BYTES),
        name="ffn_rows",
    )(x, shift, scale, gate, gpre, gpost, wg, wu, wd)
    return out[0] if w_index is None else out


def _ffn_stream_kernel(xa_ref, xc_ref, shift_ref, scale_ref, gate_ref, gpre_ref, gpost_ref, wg_ref, wu_ref,
                       wd_ref, *rest, n_tiles, n_cast):
    cast_in, rest = rest[:n_cast], rest[n_cast:]
    o_ref, cast_out = rest[0], rest[1:1 + n_cast]
    h0_ref, h1_ref, acc0_ref, acc1_ref = rest[1 + n_cast:]
    i = pl.program_id(0)
    j = pl.program_id(1)
    rc = xa_ref.shape[0]
    row0 = pl.multiple_of(j * rc, rc)
    zeros = jnp.zeros((ROW_CHUNK, acc0_ref.shape[1]), F32)

    def pre_norm(h_ref, clear_refs):
        for s in range(rc // ROW_CHUNK):
            rows = pl.ds(row0 + s * ROW_CHUNK, ROW_CHUNK)
            xn = _rms(xa_ref[s * ROW_CHUNK:(s + 1) * ROW_CHUNK, :], gpre_ref[...])
            h_ref[rows, :] = (xn * (1.0 + scale_ref[...]) + shift_ref[...]).astype(BF16)
            for acc_ref in clear_refs:
                acc_ref[rows, :] = zeros

    def post_norm(acc_ref):
        for s in range(rc // ROW_CHUNK):
            sl = slice(s * ROW_CHUNK, (s + 1) * ROW_CHUNK)
            y = gate_ref[...] * _rms(acc_ref[pl.ds(row0 + s * ROW_CHUNK, ROW_CHUNK), :], gpost_ref[...])
            o_ref[sl, :] = xc_ref[sl, :] + 0.5 * y

    @pl.when(i == 0)
    def _():
        pre_norm(h0_ref, (acc0_ref, acc1_ref))

    for parity, (h_new, acc_new, h_mm, acc_mm) in enumerate(
            ((h0_ref, acc0_ref, h1_ref, acc1_ref), (h1_ref, acc1_ref, h0_ref, acc0_ref))):
        is_mine = lax.rem(i, 2) == parity

        @pl.when(jnp.logical_and(is_mine, jnp.logical_and(i >= 1, i <= n_tiles)))
        def _():
            post_norm(acc_new)
            pre_norm(h_new, (acc_new,))
            acc_mm[...] += _swiglu_accumulate(h_mm[...], wg_ref[...], wu_ref[...], wd_ref[...])
            for src_ref, dst_ref in zip(cast_in, cast_out):
                dst_ref[...] = src_ref[...].astype(BF16)

        @pl.when(jnp.logical_and(is_mine, i == n_tiles + 1))
        def _():
            post_norm(acc_new)


def _ffn_stream(x, shift, scale, gate, gpre, gpost, wg, wu, wd, *, tm, tf, cast_next=None):
    r, d = x.shape
    f = wg.shape[-1]
    assert shift.shape == (1, d)
    nt, nj = r // tm, f // tf
    rc = tm // nj
    assert rc % ROW_CHUNK == 0 and nt >= 2
    n_steps = nt * nj

    def w_block(i, j):
        return jnp.where(i == 0, 0, jnp.where(i == nt + 1, nj - 1, j))

    cast_args, cast_in_specs, cast_out_specs, cast_shapes = [], [], [], []
    if cast_next is not None:
        *w_next, layer, sub = cast_next
        for w in w_next:
            rows, cols = w.shape[2:]
            br = max(BF16_SUBLANES, rows // n_steps)
            nb = rows // br
            assert rows % br == 0 and n_steps % nb == 0

            def blk(i, j, nb=nb):
                return (jnp.clip((i - 1) * nj + j, 0, n_steps - 1) * nb) // n_steps

            cast_args.append(w)
            cast_in_specs.append(pl.BlockSpec((None, None, br, cols), lambda i, j, blk=blk: (layer, sub, blk(i, j), 0)))
            cast_out_specs.append(pl.BlockSpec((br, cols), lambda i, j, blk=blk: (blk(i, j), 0)))
            cast_shapes.append(jax.ShapeDtypeStruct((rows, cols), BF16))

    vec_spec = pl.BlockSpec((1, d), lambda i, j: (0, 0))
    out = pl.pallas_call(
        functools.partial(_ffn_stream_kernel, n_tiles=nt, n_cast=len(cast_args)),
        grid=(nt + 2, nj),
        in_specs=[
            pl.BlockSpec((rc, d), lambda i, j: (jnp.minimum(i, nt - 1) * nj + j, 0)),
            pl.BlockSpec((rc, d), lambda i, j: (jnp.maximum(i - 2, 0) * nj + j, 0)),
            vec_spec, vec_spec, vec_spec, vec_spec, vec_spec,
            pl.BlockSpec((d, tf), lambda i, j: (0, w_block(i, j))),
            pl.BlockSpec((d, tf), lambda i, j: (0, w_block(i, j))),
            pl.BlockSpec((tf, d), lambda i, j: (w_block(i, j), 0)),
        ] + cast_in_specs,
        out_specs=[pl.BlockSpec((rc, d), lambda i, j: (jnp.maximum(i - 2, 0) * nj + jnp.where(i >= 2, j, 0), 0))]
        + cast_out_specs,
        out_shape=[jax.ShapeDtypeStruct((r, d), F32)] + cast_shapes,
        scratch_shapes=[pltpu.VMEM((tm, d), BF16), pltpu.VMEM((tm, d), BF16),
                        pltpu.VMEM((tm, d), F32), pltpu.VMEM((tm, d), F32)],
        compiler_params=pltpu.CompilerParams(
            dimension_semantics=("arbitrary", "arbitrary"), vmem_limit_bytes=VMEM_LIMIT_BYTES),
        name="ffn_stream",
    )(x, x, shift, scale, gate, gpre, gpost, wg, wu, wd, *cast_args)
    return out[0], tuple(out[1:])


def _split3(x):
    hi = x.astype(BF16)
    r1 = x - hi.astype(F32)
    mid = r1.astype(BF16)
    lo = (r1 - mid.astype(F32)).astype(BF16)
    return hi, mid, lo


def _pad_rows(a, rows):
    if a.shape[0] == rows:
        return a
    return jnp.concatenate([a, jnp.zeros((rows - a.shape[0], a.shape[1]), a.dtype)], axis=0)


def _mixer_kernel(x_ref, shift_ref, scale_ref, gate_ref, gpre_ref, gpost_ref, win_ref, wfr_ref, wfg_ref,
                  bfg_ref, wpool_ref, pscale_ref, gnorm_ref, wout_ref, pool_prev_ref, gla_prev_ref,
                  o_ref, pool_new_ref, gla_new_ref, z_ref, ext_ref, s_ref, mix_ref, *, start, tp):
    t = pl.program_id(1)
    tm, d = x_ref.shape
    pw = ext_ref.shape[1]
    gw = pw // len(POOL_WINDOWS)
    kw = wfg_ref.shape[1]
    dk = kw // GLA_HEADS
    vw = gnorm_ref.shape[1]
    dv = vw // GLA_HEADS
    q0, k0, v0, g0 = pw, pw + kw, pw + 2 * kw, pw + 2 * kw + vw
    n_blk = tp // CHUNK

    @pl.when(t == 0)
    def _():
        ext_ref[0:HIST_ROWS, :] = pool_prev_ref[...]
        s_ref[...] = gla_prev_ref[...]

    h = (_rms(x_ref[...], gpre_ref[...]) * (1.0 + scale_ref[...]) + shift_ref[...]).astype(BF16)
    z_ref[...] = _dot(h, win_ref[...])
    fr = _dot(h, wfr_ref[...])

    ext_ref[HIST_ROWS:HIST_ROWS + tm, :] = z_ref[:, 0:pw]
    pos = start + t * tm + lax.broadcasted_iota(jnp.int32, (tm, 1), 0)
    for gi, w in enumerate(POOL_WINDOWS):
        cols = slice(gi * gw, (gi + 1) * gw)
        u = ext_ref[HIST_ROWS:HIST_ROWS + tm, cols]
        wsum = u
        for back in range(1, w):
            wsum = wsum + ext_ref[HIST_ROWS - back:HIST_ROWS - back + tm, cols]
        cnt = jnp.minimum(pos + 1, w).astype(F32)
        m = (wsum / cnt - u).astype(BF16)
        y = _dot(m, wpool_ref[gi]) * pscale_ref[:, cols]
        mix_ref[:, cols] = y.astype(BF16)
    hist = ext_ref[tm:tm + HIST_ROWS, :]
    ext_ref[0:HIST_ROWS, :] = hist
    pool_new_ref[...] = hist

    logf = _log_sigmoid(_dot(fr.astype(BF16), wfg_ref[...]) + bfg_ref[...]) / GLA_TAU
    logf = _pad_rows(logf, tp)
    if tp != tm:
        logf = jnp.where(lax.broadcasted_iota(jnp.int32, (tp, 1), 0) < tm, logf, 0.0)
    row = lax.broadcasted_iota(jnp.int32, (tp, tp), 0)
    col = lax.broadcasted_iota(jnp.int32, (tp, tp), 1)
    causal = jnp.logical_and(row >> CHUNK_SHIFT == col >> CHUNK_SHIFT, col <= row)
    tril = jnp.where(causal, 1.0, 0.0).astype(BF16)
    hi, mid, lo = _split3(logf)
    b = _dot(tril, hi) + _dot(tril, mid) + _dot(tril, lo)

    for hh in range(GLA_HEADS):
        kc = slice(hh * dk, (hh + 1) * dk)
        q = _pad_rows(z_ref[:, q0 + hh * dk:q0 + (hh + 1) * dk], tp)
        k = _pad_rows(z_ref[:, k0 + hh * dk:k0 + (hh + 1) * dk], tp)
        v = _pad_rows(z_ref[:, v0 + hh * dv:v0 + (hh + 1) * dv], tp).astype(BF16)
        bh = b[:, kc]
        qe = (q * jnp.exp(bh) * (dk ** -0.5)).astype(BF16)
        ke = (k * jnp.exp(-bh)).astype(BF16)
        att = jnp.where(causal, _dot_nt(qe, ke), 0.0).astype(BF16)
        o = _dot(att, v)
        kv, decay = [], []
        for c in range(n_blk):
            rows = slice(c * CHUNK, (c + 1) * CHUNK)
            bt = bh[rows].T
            b_last = bt[:, CHUNK - 1:CHUNK]
            kd_t = (k[rows].T * jnp.exp(b_last - bt)).astype(BF16)
            kv.append(_dot(kd_t, v[rows]))
            decay.append(jnp.exp(b_last))
        state = s_ref[hh]
        o_blocks = []
        for c in range(n_blk):
            rows = slice(c * CHUNK, (c + 1) * CHUNK)
            o_blocks.append(_dot(qe[rows], state.astype(BF16)))
            state = decay[c] * state + kv[c]
        s_ref[hh] = state
        gla_new_ref[hh] = state
        o = o + jnp.concatenate(o_blocks, axis=0) if n_blk > 1 else o + o_blocks[0]
        o = _rms(o[0:tm], gnorm_ref[:, hh * dv:(hh + 1) * dv])
        og = z_ref[:, g0 + hh * dv:g0 + (hh + 1) * dv]
        mix_ref[:, pw + hh * dv:pw + (hh + 1) * dv] = (o * _silu(og)).astype(BF16)

    y = _dot(mix_ref[...], wout_ref[...])
    o_ref[...] = x_ref[...] + gate_ref[...] * _rms(y, gpost_ref[...])


def _mixer(x, shift, scale, gate, gpre, gpost, w_in, w_fr, w_fg, b_fg, w_pool, pscale, gnorm, w_out,
           pool_prev, gla_prev, layer, *, n_seq, tm, start):
    r, d = x.shape
    n_tiles = r // n_seq // tm
    tp = max(tm, CHUNK)
    zw = w_in.shape[-1]
    pw = pool_prev.shape[-1]
    hds, dk, dv = gla_prev.shape[1:]
    const = dict(pipeline_mode=pl.Buffered(1))

    def seq_spec(shape):
        return pl.BlockSpec((None,) + shape, lambda s, t: (s,) + (0,) * len(shape))

    def layer_spec(shape):
        return pl.BlockSpec((None,) + shape, lambda s, t: (layer,) + (0,) * len(shape), **const)

    row_spec = pl.BlockSpec((tm, d), lambda s, t: (s * n_tiles + t, 0))
    return pl.pallas_call(
        functools.partial(_mixer_kernel, start=start, tp=tp),
        grid=(n_seq, n_tiles),
        in_specs=[
            row_spec,
            seq_spec((1, d)), seq_spec((1, d)), seq_spec((1, d)),
            pl.BlockSpec((1, d), lambda s, t: (0, 0)), pl.BlockSpec((1, d), lambda s, t: (0, 0)),
            layer_spec((d, zw)), layer_spec((d, GLA_RANK_PAD)), layer_spec((GLA_RANK_PAD, hds * dk)),
            layer_spec((1, hds * dk)), layer_spec(w_pool.shape[1:]), layer_spec((1, pw)),
            layer_spec((1, hds * dv)), layer_spec((pw + hds * dv, d)),
            seq_spec((HIST_ROWS, pw)), seq_spec((hds, dk, dv)),
        ],
        out_specs=[row_spec, seq_spec((HIST_ROWS, pw)), seq_spec((hds, dk, dv))],
        out_shape=[
            jax.ShapeDtypeStruct((r, d), F32),
            jax.ShapeDtypeStruct((n_seq, HIST_ROWS, pw), F32),
            jax.ShapeDtypeStruct((n_seq, hds, dk, dv), F32),
        ],
        scratch_shapes=[
            pltpu.VMEM((tm, zw), F32),
            pltpu.VMEM((HIST_ROWS + tm, pw), F32),
            pltpu.VMEM((hds, dk, dv), F32),
            pltpu.VMEM((tm, pw + hds * dv), BF16),
        ],
        compiler_params=pltpu.CompilerParams(
            dimension_semantics=("arbitrary", "arbitrary"), vmem_limit_bytes=VMEM_LIMIT_BYTES),
        name="mixer",
    )(x, shift, scale, gate, gpre, gpost, w_in, w_fr, w_fg, b_fg, w_pool, pscale, gnorm, w_out,
      pool_prev, gla_prev)


PROMPT_ROW_TILE = 512
PROMPT_FF_TILE = 1024
SAMPLE_FF_TILE = 512
MIXER_ROW_TILE = 256


def kernel(x_prompt, x_sample, state_pool, state_gla, c_prompt, c_sample, w_ada, b_ada, norm_pre, norm_post,
           w_ffn_gate, w_ffn_up, w_ffn_down, w_in, w_forget, b_forget, w_pool, pool_scale, gla_norm, w_out):
    bp, seq, d = x_prompt.shape
    bs, dec_seq, _ = x_sample.shape
    depth = w_ada.shape[0]
    pw = state_pool.shape[-1]
    hds, dk, dv = state_gla.shape[2:]
    kw, vw = hds * dk, hds * dv
    zw = pw + 2 * kw + 2 * vw
    rank = w_forget.shape[1]
    d_ff = w_ffn_gate.shape[-1]

    n_c = bp + bs
    c_rows = -(-n_c // 16) * 16
    c_all = jnp.concatenate([c_prompt, c_sample, jnp.zeros((c_rows - n_c, d), F32)], axis=0)
    mod = _ada(c_all, w_ada, b_ada).reshape(depth, c_rows, N_SUB, 3, d)

    w_in_main = w_in[:, :, :zw].astype(BF16)
    w_fr = jnp.pad(w_in[:, :, zw:], ((0, 0), (0, 0), (0, GLA_RANK_PAD - rank))).astype(BF16)
    w_fg = jnp.pad(w_forget, ((0, 0), (0, GLA_RANK_PAD - rank), (0, 0))).astype(BF16)
    w_pool_b, w_out_b = w_pool.astype(BF16), w_out.astype(BF16)

    xp = x_prompt.reshape(bp * seq, d)
    xs = x_sample.reshape(bs * dec_seq, d)
    pool0 = jnp.zeros((bp, HIST_ROWS, pw), F32)
    gla0 = jnp.zeros((bp, hds, dk, dv), F32)
    pool_s_in = jnp.pad(state_pool, ((0, 0), (0, 0), (1, 0), (0, 0)))
    tm_p = min(PROMPT_ROW_TILE, bp * seq // 2)
    tf_p, tf_s = min(PROMPT_FF_TILE, d_ff), min(SAMPLE_FF_TILE, d_ff)
    tm_mix = min(seq, MIXER_ROW_TILE)

    outs = ([], [], [], [])
    w_b = None
    for l in range(depth):
        gpre, gpost = norm_pre[l][:, None, :], norm_post[l][:, None, :]
        mp = mod[l, :bp]
        ms = mod[l, bp:n_c]
        ms_rows = jnp.repeat(ms, dec_seq, axis=0)
        lw = (w_in_main, w_fr, w_fg, b_forget[:, None, :], w_pool_b, pool_scale[:, None, :],
              gla_norm[:, None, :], w_out_b)

        def ffn(xp, xs, w_b, sub, i):
            mods = (ms_rows[:, i, 0], ms_rows[:, i, 1], ms_rows[:, i, 2], gpre[i], gpost[i])
            if w_b is None:
                xs, *w_b = _ffn_rows(xs, *mods, w_ffn_gate, w_ffn_up, w_ffn_down, tf=tf_s, w_index=(l, sub))
            else:
                xs = _ffn_rows(xs, *mods, *w_b, tf=tf_p)
            nxt = (l, 1) if sub == 0 else (l + 1, 0)
            cast_next = (w_ffn_gate, w_ffn_up, w_ffn_down) + nxt if nxt[0] < depth else None
            xp, w_next = _ffn_stream(xp, mp[:, i, 0], mp[:, i, 1], mp[:, i, 2], gpre[i], gpost[i], *w_b,
                                     tm=tm_p, tf=tf_p, cast_next=cast_next)
            return xp, xs, (w_next or None)

        xp, xs, w_b = ffn(xp, xs, w_b, 0, 0)
        xp, pool_p, gla_p = _mixer(xp, mp[:, 1, 0][:, None], mp[:, 1, 1][:, None], mp[:, 1, 2][:, None],
                                   gpre[1], gpost[1], *lw, pool0, gla0, l, n_seq=bp, tm=tm_mix, start=0)
        xs, pool_s, gla_s = _mixer(xs, ms[:, 1, 0][:, None], ms[:, 1, 1][:, None], ms[:, 1, 2][:, None],
                                   gpre[1], gpost[1], *lw, pool_s_in[l], state_gla[l], l, n_seq=bs, tm=dec_seq,
                                   start=PAST_LEN)
        xp, xs, w_b = ffn(xp, xs, w_b, 1, 2)
        for acc, val in zip(outs, (pool_p[:, 1:], gla_p, pool_s[:, 1:], gla_s)):
            acc.append(val)

    return (xp.reshape(bp, seq, d), xs.reshape(bs, dec_seq, d),
            jnp.stack(outs[0]), jnp.stack(outs[1]), jnp.stack(outs[2]), jnp.stack(outs[3]))
```

```python
import functools

import jax
import jax.numpy as jnp
from jax import lax
from jax.experimental import pallas as pl
from jax.experimental.pallas import tpu as pltpu

F32 = jnp.float32
BF16 = jnp.bfloat16

EPS = 1e-6
N_SUB = 3
CHUNK = 64
CHUNK_SHIFT = CHUNK.bit_length() - 1
assert CHUNK == 1 << CHUNK_SHIFT
POOL_WINDOWS = (2, 4, 8, 16)
POOL_HIST = max(POOL_WINDOWS) - 1
HIST_ROWS = POOL_HIST + 1
GLA_HEADS = 4
GLA_TAU = 16.0
GLA_RANK_PAD = 128
PAST_LEN = 4096

ROW_CHUNK = 16
BF16_SUBLANES = 16
CAST_EVERY = 4
VMEM_LIMIT_BYTES = 56 * 1024 * 1024


def _silu(x):
    return x * jax.nn.sigmoid(x)


def _log_sigmoid(x):
    return jnp.minimum(x, 0.0) - jnp.log1p(jnp.exp(-jnp.abs(x)))


def _rms(x, g):
    return x * lax.rsqrt(jnp.mean(x * x, axis=-1, keepdims=True) + EPS) * g


def _dot(a, b):
    return jnp.dot(a, b, preferred_element_type=F32)


def _dot_nt(a, b):
    return lax.dot_general(a, b, (((1,), (1,)), ((), ())), preferred_element_type=F32)


def _ada_kernel(c_ref, w_ref, b_ref, o_ref):
    c = c_ref[...]
    o_ref[0] = _dot(_silu(c).astype(BF16), w_ref[0].astype(BF16)) + b_ref[0]


def _ada(c_all, w_ada, b_ada, tn=1024):
    depth, d, n = w_ada.shape
    rows = c_all.shape[0]
    return pl.pallas_call(
        _ada_kernel,
        grid=(depth, n // tn),
        in_specs=[
            pl.BlockSpec((rows, d), lambda l, j: (0, 0)),
            pl.BlockSpec((1, d, tn), lambda l, j: (l, 0, j)),
            pl.BlockSpec((1, 1, tn), lambda l, j: (l, 0, j)),
        ],
        out_specs=pl.BlockSpec((1, rows, tn), lambda l, j: (l, 0, j)),
        out_shape=jax.ShapeDtypeStruct((depth, rows, n), F32),
        compiler_params=pltpu.CompilerParams(
            dimension_semantics=("arbitrary", "arbitrary"), vmem_limit_bytes=VMEM_LIMIT_BYTES),
        name="ada",
    )(c_all, w_ada, b_ada.reshape(depth, 1, n))


def _swiglu_accumulate(h, wg, wu, wd):
    return _dot((_silu(_dot(h, wg)) * _dot(h, wu)).astype(BF16), wd)


def _ffn_rows_kernel(x_ref, shift_ref, scale_ref, gate_ref, gpre_ref, gpost_ref, wg_ref, wu_ref, wd_ref,
                     o_ref, *rest):
    h_ref = rest[-1]
    j = pl.program_id(0)
    n_chunks = x_ref.shape[0] // ROW_CHUNK

    @pl.when(j == 0)
    def _():
        def body(r, carry):
            sl = pl.ds(pl.multiple_of(r * ROW_CHUNK, ROW_CHUNK), ROW_CHUNK)
            xn = _rms(x_ref[sl, :], gpre_ref[...])
            h_ref[sl, :] = (xn * (1.0 + scale_ref[sl, :]) + shift_ref[sl, :]).astype(BF16)
            o_ref[sl, :] = jnp.zeros((ROW_CHUNK, o_ref.shape[1]), F32)
            return carry
        lax.fori_loop(0, n_chunks, body, 0)

    wg, wu, wd = wg_ref[...].astype(BF16), wu_ref[...].astype(BF16), wd_ref[...].astype(BF16)
    for copy_ref, w in zip(rest[:-1], (wg, wu, wd)):
        copy_ref[...] = w
    o_ref[...] += _swiglu_accumulate(h_ref[...], wg, wu, wd)

    @pl.when(j == pl.num_programs(0) - 1)
    def _():
        def body(r, carry):
            sl = pl.ds(pl.multiple_of(r * ROW_CHUNK, ROW_CHUNK), ROW_CHUNK)
            y = gate_ref[sl, :] * _rms(o_ref[sl, :], gpost_ref[...])
            o_ref[sl, :] = x_ref[sl, :] + 0.5 * y
            return carry
        lax.fori_loop(0, n_chunks, body, 0)


def _ffn_rows(x, shift, scale, gate, gpre, gpost, wg, wu, wd, *, tf, w_index=None):
    r, d = x.shape
    f = wg.shape[-1]
    row_spec = pl.BlockSpec((r, d), lambda j: (0, 0))
    vec_spec = pl.BlockSpec((1, d), lambda j: (0, 0))
    w_copy_specs = [pl.BlockSpec((d, tf), lambda j: (0, j)), pl.BlockSpec((d, tf), lambda j: (0, j)),
                    pl.BlockSpec((tf, d), lambda j: (j, 0))]
    if w_index is None:
        w_specs, copy_specs, copy_shapes = w_copy_specs, [], []
    else:
        layer, sub = w_index
        w_specs = [pl.BlockSpec((None, None, d, tf), lambda j: (layer, sub, 0, j)),
                   pl.BlockSpec((None, None, d, tf), lambda j: (layer, sub, 0, j)),
                   pl.BlockSpec((None, None, tf, d), lambda j: (layer, sub, j, 0))]
        copy_specs = w_copy_specs
        copy_shapes = [jax.ShapeDtypeStruct((d, f), BF16), jax.ShapeDtypeStruct((d, f), BF16),
                       jax.ShapeDtypeStruct((f, d), BF16)]
    out = pl.pallas_call(
        _ffn_rows_kernel,
        grid=(f // tf,),
        in_specs=[row_spec, row_spec, row_spec, row_spec, vec_spec, vec_spec] + w_specs,
        out_specs=[row_spec] + copy_specs,
        out_shape=[jax.ShapeDtypeStruct((r, d), F32)] + copy_shapes,
        scratch_shapes=[pltpu.VMEM((r, d), BF16)],
        compiler_params=pltpu.CompilerParams(
            dimension_semantics=("arbitrary",), vmem_limit_bytes=VMEM_LIMIT_BYTES),
        name="ffn_rows",
    )(x, shift, scale, gate, gpre, gpost, wg, wu, wd)
    return out[0] if w_index is None else out


def _ffn_stream_kernel(xa_ref, xc_ref, shift_ref, scale_ref, gate_ref, gpre_ref, gpost_ref, wg_ref, wu_ref,
                       wd_ref, *rest, n_tiles, n_cast, cast_every):
    cast_in, rest = rest[:n_cast], rest[n_cast:]
    o_ref, cast_out = rest[0], rest[1:1 + n_cast]
    h0_ref, h1_ref, acc0_ref, acc1_ref = rest[1 + n_cast:]
    i = pl.program_id(0)
    j = pl.program_id(1)
    rc = xa_ref.shape[0]
    row0 = pl.multiple_of(j * rc, rc)
    zeros = jnp.zeros((ROW_CHUNK, acc0_ref.shape[1]), F32)

    def pre_norm(h_ref, clear_refs):
        for s in range(rc // ROW_CHUNK):
            rows = pl.ds(row0 + s * ROW_CHUNK, ROW_CHUNK)
            xn = _rms(xa_ref[s * ROW_CHUNK:(s + 1) * ROW_CHUNK, :], gpre_ref[...])
            h_ref[rows, :] = (xn * (1.0 + scale_ref[...]) + shift_ref[...]).astype(BF16)
            for acc_ref in clear_refs:
                acc_ref[rows, :] = zeros

    def post_norm(acc_ref):
        for s in range(rc // ROW_CHUNK):
            sl = slice(s * ROW_CHUNK, (s + 1) * ROW_CHUNK)
            y = gate_ref[...] * _rms(acc_ref[pl.ds(row0 + s * ROW_CHUNK, ROW_CHUNK), :], gpost_ref[...])
            o_ref[sl, :] = xc_ref[sl, :] + 0.5 * y

    @pl.when(i == 0)
    def _():
        pre_norm(h0_ref, (acc0_ref, acc1_ref))

    for parity, (h_new, acc_new, h_mm, acc_mm) in enumerate(
            ((h0_ref, acc0_ref, h1_ref, acc1_ref), (h1_ref, acc1_ref, h0_ref, acc0_ref))):
        is_mine = lax.rem(i, 2) == parity

        @pl.when(jnp.logical_and(is_mine, jnp.logical_and(i >= 1, i <= n_tiles)))
        def _():
            post_norm(acc_new)
            pre_norm(h_new, (acc_new,))
            acc_mm[...] += _swiglu_accumulate(h_mm[...], wg_ref[...], wu_ref[...], wd_ref[...])

        @pl.when(jnp.logical_and(is_mine, i == n_tiles + 1))
        def _():
            post_norm(acc_new)

    if n_cast:
        @pl.when(jnp.logical_and(lax.rem(j, cast_every) == 0, jnp.logical_and(i >= 1, i <= n_tiles)))
        def _():
            for src_ref, dst_ref in zip(cast_in, cast_out):
                dst_ref[...] = src_ref[...].astype(BF16)


def _ffn_stream(x, shift, scale, gate, gpre, gpost, wg, wu, wd, *, tm, tf, cast_next=None):
    r, d = x.shape
    f = wg.shape[-1]
    assert shift.shape == (1, d)
    nt, nj = r // tm, f // tf
    rc = tm // nj
    assert rc % ROW_CHUNK == 0 and nt >= 2
    n_steps = nt * nj

    def w_block(i, j):
        return jnp.where(i == 0, 0, jnp.where(i == nt + 1, nj - 1, j))

    cast_args, cast_in_specs, cast_out_specs, cast_shapes = [], [], [], []
    cast_every = CAST_EVERY if nj % CAST_EVERY == 0 else 1
    n_casts = n_steps // cast_every
    if cast_next is not None:
        *w_next, layer, sub = cast_next
        for w in w_next:
            rows, cols = w.shape[2:]
            br = max(BF16_SUBLANES, rows // n_casts)
            nb = rows // br
            assert rows % br == 0 and n_casts % nb == 0

            def blk(i, j, nb=nb):
                return (jnp.clip((i - 1) * nj + j, 0, n_steps - 1) // cast_every * nb) // n_casts

            cast_args.append(w)
            cast_in_specs.append(pl.BlockSpec((None, None, br, cols), lambda i, j, blk=blk: (layer, sub, blk(i, j), 0)))
            cast_out_specs.append(pl.BlockSpec((br, cols), lambda i, j, blk=blk: (blk(i, j), 0)))
            cast_shapes.append(jax.ShapeDtypeStruct((rows, cols), BF16))

    vec_spec = pl.BlockSpec((1, d), lambda i, j: (0, 0))
    out = pl.pallas_call(
        functools.partial(_ffn_stream_kernel, n_tiles=nt, n_cast=len(cast_args), cast_every=cast_every),
        grid=(nt + 2, nj),
        in_specs=[
            pl.BlockSpec((rc, d), lambda i, j: (jnp.minimum(i, nt - 1) * nj + j, 0)),
            pl.BlockSpec((rc, d), lambda i, j: (jnp.maximum(i - 2, 0) * nj + j, 0)),
            vec_spec, vec_spec, vec_spec, vec_spec, vec_spec,
            pl.BlockSpec((d, tf), lambda i, j: (0, w_block(i, j))),
            pl.BlockSpec((d, tf), lambda i, j: (0, w_block(i, j))),
            pl.BlockSpec((tf, d), lambda i, j: (w_block(i, j), 0)),
        ] + cast_in_specs,
        out_specs=[pl.BlockSpec((rc, d), lambda i, j: (jnp.maximum(i - 2, 0) * nj + jnp.where(i >= 2, j, 0), 0))]
        + cast_out_specs,
        out_shape=[jax.ShapeDtypeStruct((r, d), F32)] + cast_shapes,
        scratch_shapes=[pltpu.VMEM((tm, d), BF16), pltpu.VMEM((tm, d), BF16),
                        pltpu.VMEM((tm, d), F32), pltpu.VMEM((tm, d), F32)],
        compiler_params=pltpu.CompilerParams(
            dimension_semantics=("arbitrary", "arbitrary"), vmem_limit_bytes=VMEM_LIMIT_BYTES),
        name="ffn_stream",
    )(x, x, shift, scale, gate, gpre, gpost, wg, wu, wd, *cast_args)
    return out[0], tuple(out[1:])


def _split3(x):
    hi = x.astype(BF16)
    r1 = x - hi.astype(F32)
    mid = r1.astype(BF16)
    lo = (r1 - mid.astype(F32)).astype(BF16)
    return hi, mid, lo


def _pad_rows(a, rows):
    if a.shape[0] == rows:
        return a
    return jnp.concatenate([a, jnp.zeros((rows - a.shape[0], a.shape[1]), a.dtype)], axis=0)


def _mixer_kernel(x_ref, shift_ref, scale_ref, gate_ref, gpre_ref, gpost_ref, win_ref, wfr_ref, wfg_ref,
                  bfg_ref, wpool_ref, pscale_ref, gnorm_ref, wout_ref, pool_prev_ref, gla_prev_ref,
                  o_ref, pool_new_ref, gla_new_ref, z_ref, ext_ref, s_ref, mix_ref, *, start, tp):
    t = pl.program_id(1)
    tm, d = x_ref.shape
    pw = ext_ref.shape[1]
    gw = pw // len(POOL_WINDOWS)
    kw = wfg_ref.shape[1]
    dk = kw // GLA_HEADS
    vw = gnorm_ref.shape[1]
    dv = vw // GLA_HEADS
    q0, k0, v0, g0 = pw, pw + kw, pw + 2 * kw, pw + 2 * kw + vw
    n_blk = tp // CHUNK

    @pl.when(t == 0)
    def _():
        ext_ref[0:HIST_ROWS, :] = pool_prev_ref[...]
        s_ref[...] = gla_prev_ref[...]

    row = lax.broadcasted_iota(jnp.int32, (tp, tp), 0)
    col = lax.broadcasted_iota(jnp.int32, (tp, tp), 1)
    causal = jnp.logical_and(row >> CHUNK_SHIFT == col >> CHUNK_SHIFT, col <= row)
    tril = jnp.where(causal, 1.0, 0.0).astype(BF16)

    h = (_rms(x_ref[...], gpre_ref[...]) * (1.0 + scale_ref[...]) + shift_ref[...]).astype(BF16)
    fr = _dot(h, wfr_ref[...])
    z_ref[:, 0:q0] = _dot(h, win_ref[:, 0:q0])
    logf = _log_sigmoid(_dot(fr.astype(BF16), wfg_ref[...]) + bfg_ref[...]) / GLA_TAU
    z_ref[:, q0:v0] = _dot(h, win_ref[:, q0:v0])
    logf = _pad_rows(logf, tp)
    if tp != tm:
        logf = jnp.where(lax.broadcasted_iota(jnp.int32, (tp, 1), 0) < tm, logf, 0.0)
    hi, mid, lo = _split3(logf)
    b = _dot(tril, hi) + _dot(tril, mid) + _dot(tril, lo)
    z_ref[:, v0:] = _dot(h, win_ref[:, v0:])

    ext_ref[HIST_ROWS:HIST_ROWS + tm, :] = z_ref[:, 0:pw]
    pos = start + t * tm + lax.broadcasted_iota(jnp.int32, (tm, 1), 0)
    for gi, w in enumerate(POOL_WINDOWS):
        cols = slice(gi * gw, (gi + 1) * gw)
        u = ext_ref[HIST_ROWS:HIST_ROWS + tm, cols]
        wsum = u
        for back in range(1, w):
            wsum = wsum + ext_ref[HIST_ROWS - back:HIST_ROWS - back + tm, cols]
        cnt = jnp.minimum(pos + 1, w).astype(F32)
        m = (wsum / cnt - u).astype(BF16)
        y = _dot(m, wpool_ref[gi]) * pscale_ref[:, cols]
        mix_ref[:, cols] = y.astype(BF16)
    hist = ext_ref[tm:tm + HIST_ROWS, :]
    ext_ref[0:HIST_ROWS, :] = hist
    pool_new_ref[...] = hist

    heads = range(GLA_HEADS)
    blocks = [slice(c * CHUNK, (c + 1) * CHUNK) for c in range(n_blk)]
    k, v, bh, qe, att = [], [], [], [], []
    for hh in heads:
        q = _pad_rows(z_ref[:, q0 + hh * dk:q0 + (hh + 1) * dk], tp)
        k.append(_pad_rows(z_ref[:, k0 + hh * dk:k0 + (hh + 1) * dk], tp))
        v.append(_pad_rows(z_ref[:, v0 + hh * dv:v0 + (hh + 1) * dv], tp).astype(BF16))
        bh.append(b[:, hh * dk:(hh + 1) * dk])
        qe.append((q * jnp.exp(bh[hh]) * (dk ** -0.5)).astype(BF16))
        ke = (k[hh] * jnp.exp(-bh[hh])).astype(BF16)
        att.append(jnp.where(causal, _dot_nt(qe[hh], ke), 0.0).astype(BF16))
    o = [_dot(att[hh], v[hh]) for hh in heads]
    kv, decay = [], []
    for hh in heads:
        kv.append([])
        decay.append([])
        for rows in blocks:
            bt = bh[hh][rows].T
            b_last = bt[:, CHUNK - 1:CHUNK]
            kd_t = (k[hh][rows].T * jnp.exp(b_last - bt)).astype(BF16)
            kv[hh].append(_dot(kd_t, v[hh][rows]))
            decay[hh].append(jnp.exp(b_last))
    state = [s_ref[hh] for hh in heads]
    o_state = [[] for _ in heads]
    for c, rows in enumerate(blocks):
        for hh in heads:
            o_state[hh].append(_dot(qe[hh][rows], state[hh].astype(BF16)))
            state[hh] = decay[hh][c] * state[hh] + kv[hh][c]
    for hh in heads:
        s_ref[hh] = state[hh]
        gla_new_ref[hh] = state[hh]
        o_h = o[hh] + (jnp.concatenate(o_state[hh], axis=0) if n_blk > 1 else o_state[hh][0])
        o_h = _rms(o_h[0:tm], gnorm_ref[:, hh * dv:(hh + 1) * dv])
        og = z_ref[:, g0 + hh * dv:g0 + (hh + 1) * dv]
        mix_ref[:, pw + hh * dv:pw + (hh + 1) * dv] = (o_h * _silu(og)).astype(BF16)

    y = _dot(mix_ref[...], wout_ref[...])
    o_ref[...] = x_ref[...] + gate_ref[...] * _rms(y, gpost_ref[...])


def _mixer(x, shift, scale, gate, gpre, gpost, w_in, w_fr, w_fg, b_fg, w_pool, pscale, gnorm, w_out,
           pool_prev, gla_prev, layer, *, n_seq, tm, start):
    r, d = x.shape
    n_tiles = r // n_seq // tm
    tp = max(tm, CHUNK)
    pw = pool_prev.shape[-1]
    hds, dk, dv = gla_prev.shape[1:]
    zw = pw + 2 * hds * (dk + dv)
    const = dict(pipeline_mode=pl.Buffered(1))

    def seq_spec(shape):
        return pl.BlockSpec((None,) + shape, lambda s, t: (s,) + (0,) * len(shape))

    def layer_spec(shape):
        return pl.BlockSpec((None,) + shape, lambda s, t: (layer,) + (0,) * len(shape), **const)

    row_spec = pl.BlockSpec((tm, d), lambda s, t: (s * n_tiles + t, 0))
    return pl.pallas_call(
        functools.partial(_mixer_kernel, start=start, tp=tp),
        grid=(n_seq, n_tiles),
        in_specs=[
            row_spec,
            seq_spec((1, d)), seq_spec((1, d)), seq_spec((1, d)),
            pl.BlockSpec((1, d), lambda s, t: (0, 0)), pl.BlockSpec((1, d), lambda s, t: (0, 0)),
            layer_spec((d, zw)), layer_spec((d, GLA_RANK_PAD)), layer_spec((GLA_RANK_PAD, hds * dk)),
            layer_spec((1, hds * dk)), layer_spec(w_pool.shape[1:]), layer_spec((1, pw)),
            layer_spec((1, hds * dv)), layer_spec((pw + hds * dv, d)),
            seq_spec((HIST_ROWS, pw)), seq_spec((hds, dk, dv)),
        ],
        out_specs=[row_spec, seq_spec((HIST_ROWS, pw)), seq_spec((hds, dk, dv))],
        out_shape=[
            jax.ShapeDtypeStruct((r, d), F32),
            jax.ShapeDtypeStruct((n_seq, HIST_ROWS, pw), F32),
            jax.ShapeDtypeStruct((n_seq, hds, dk, dv), F32),
        ],
        scratch_shapes=[
            pltpu.VMEM((tm, zw), F32),
            pltpu.VMEM((HIST_ROWS + tm, pw), F32),
            pltpu.VMEM((hds, dk, dv), F32),
            pltpu.VMEM((tm, pw + hds * dv), BF16),
        ],
        compiler_params=pltpu.CompilerParams(
            dimension_semantics=("arbitrary", "arbitrary"), vmem_limit_bytes=VMEM_LIMIT_BYTES),
        name="mixer",
    )(x, shift, scale, gate, gpre, gpost, w_in, w_fr, w_fg, b_fg, w_pool, pscale, gnorm, w_out,
      pool_prev, gla_prev)


PROMPT_ROW_TILE = 512
PROMPT_FF_TILE = 1024
SAMPLE_FF_TILE = 512
MIXER_ROW_TILE = 256


def kernel(x_prompt, x_sample, state_pool, state_gla, c_prompt, c_sample, w_ada, b_ada, norm_pre, norm_post,
           w_ffn_gate, w_ffn_up, w_ffn_down, w_in, w_forget, b_forget, w_pool, pool_scale, gla_norm, w_out):
    bp, seq, d = x_prompt.shape
    bs, dec_seq, _ = x_sample.shape
    depth = w_ada.shape[0]
    pw = state_pool.shape[-1]
    hds, dk, dv = state_gla.shape[2:]
    kw, vw = hds * dk, hds * dv
    zw = pw + 2 * kw + 2 * vw
    rank = w_forget.shape[1]
    d_ff = w_ffn_gate.shape[-1]

    n_c = bp + bs
    c_rows = -(-n_c // 16) * 16
    c_all = jnp.concatenate([c_prompt, c_sample, jnp.zeros((c_rows - n_c, d), F32)], axis=0)
    mod = _ada(c_all, w_ada, b_ada).reshape(depth, c_rows, N_SUB, 3, d)

    w_in_b = w_in.astype(BF16)
    w_fr = jnp.pad(w_in[:, :, zw:], ((0, 0), (0, 0), (0, GLA_RANK_PAD - rank))).astype(BF16)
    w_fg = jnp.pad(w_forget, ((0, 0), (0, GLA_RANK_PAD - rank), (0, 0))).astype(BF16)
    w_pool_b, w_out_b = w_pool.astype(BF16), w_out.astype(BF16)

    xp = x_prompt.reshape(bp * seq, d)
    xs = x_sample.reshape(bs * dec_seq, d)
    pool0 = jnp.zeros((bp, HIST_ROWS, pw), F32)
    gla0 = jnp.zeros((bp, hds, dk, dv), F32)
    pool_s_in = jnp.pad(state_pool, ((0, 0), (0, 0), (1, 0), (0, 0)))
    tm_p = min(PROMPT_ROW_TILE, bp * seq // 2)
    tf_p, tf_s = min(PROMPT_FF_TILE, d_ff), min(SAMPLE_FF_TILE, d_ff)
    tm_mix = min(seq, MIXER_ROW_TILE)

    outs = ([], [], [], [])
    w_b = None
    for l in range(depth):
        gpre, gpost = norm_pre[l][:, None, :], norm_post[l][:, None, :]
        mp = mod[l, :bp]
        ms = mod[l, bp:n_c]
        ms_rows = jnp.repeat(ms, dec_seq, axis=0)
        lw = (w_in_b, w_fr, w_fg, b_forget[:, None, :], w_pool_b, pool_scale[:, None, :],
              gla_norm[:, None, :], w_out_b)

        def ffn(xp, xs, w_b, sub, i):
            mods = (ms_rows[:, i, 0], ms_rows[:, i, 1], ms_rows[:, i, 2], gpre[i], gpost[i])
            if w_b is None:
                xs, *w_b = _ffn_rows(xs, *mods, w_ffn_gate, w_ffn_up, w_ffn_down, tf=tf_s, w_index=(l, sub))
            else:
                xs = _ffn_rows(xs, *mods, *w_b, tf=tf_p)
            nxt = (l, 1) if sub == 0 else (l + 1, 0)
            cast_next = (w_ffn_gate, w_ffn_up, w_ffn_down) + nxt if nxt[0] < depth else None
            xp, w_next = _ffn_stream(xp, mp[:, i, 0], mp[:, i, 1], mp[:, i, 2], gpre[i], gpost[i], *w_b,
                                     tm=tm_p, tf=tf_p, cast_next=cast_next)
            return xp, xs, (w_next or None)

        xp, xs, w_b = ffn(xp, xs, w_b, 0, 0)
        xp, pool_p, gla_p = _mixer(xp, mp[:, 1, 0][:, None], mp[:, 1, 1][:, None], mp[:, 1, 2][:, None],
                                   gpre[1], gpost[1], *lw, pool0, gla0, l, n_seq=bp, tm=tm_mix, start=0)
        xs, pool_s, gla_s = _mixer(xs, ms[:, 1, 0][:, None], ms[:, 1, 1][:, None], ms[:, 1, 2][:, None],
                                   gpre[1], gpost[1], *lw, pool_s_in[l], state_gla[l], l, n_seq=bs, tm=dec_seq,
                                   start=PAST_LEN)
        xp, xs, w_b = ffn(xp, xs, w_b, 1, 2)
        for acc, val in zip(outs, (pool_p[:, 1:], gla_p, pool_s[:, 1:], gla_s)):
            acc.append(val)

    return (xp.reshape(bp, seq, d), xs.reshape(bs, dec_seq, d),
            jnp.stack(outs[0]), jnp.stack(outs[1]), jnp.stack(outs[2]), jnp.stack(outs[3]))
```

```python
import functools

import jax
import jax.numpy as jnp
from jax import lax
from jax.experimental import pallas as pl
from jax.experimental.pallas import tpu as pltpu

F32 = jnp.float32
BF16 = jnp.bfloat16

EPS = 1e-6
N_SUB = 3
CHUNK = 64
CHUNK_SHIFT = CHUNK.bit_length() - 1
assert CHUNK == 1 << CHUNK_SHIFT
POOL_WINDOWS = (2, 4, 8, 16)
POOL_HIST = max(POOL_WINDOWS) - 1
HIST_ROWS = POOL_HIST + 1
GLA_HEADS = 4
GLA_TAU = 16.0
GLA_RANK_PAD = 128
PAST_LEN = 4096

ROW_CHUNK = 16
BF16_SUBLANES = 16
CAST_EVERY = 4
ADA_COL_TILE = 2304
VMEM_LIMIT_BYTES = 56 * 1024 * 1024


def _silu(x):
    return x * jax.nn.sigmoid(x)


def _log_sigmoid(x):
    return jnp.minimum(x, 0.0) - jnp.log1p(jnp.exp(-jnp.abs(x)))


def _rms(x, g):
    return x * lax.rsqrt(jnp.mean(x * x, axis=-1, keepdims=True) + EPS) * g


def _dot(a, b):
    return jnp.dot(a, b, preferred_element_type=F32)


def _dot_nt(a, b):
    return lax.dot_general(a, b, (((1,), (1,)), ((), ())), preferred_element_type=F32)


def _ada_kernel(c_ref, w_ref, b_ref, o_ref):
    c = c_ref[...]
    o_ref[0] = _dot(_silu(c).astype(BF16), w_ref[0].astype(BF16)) + b_ref[0]


def _ada(c_all, w_ada, b_ada):
    depth, d, n = w_ada.shape
    rows = c_all.shape[0]
    tn = ADA_COL_TILE if n % ADA_COL_TILE == 0 else n
    return pl.pallas_call(
        _ada_kernel,
        grid=(depth, n // tn),
        in_specs=[
            pl.BlockSpec((rows, d), lambda l, j: (0, 0)),
            pl.BlockSpec((1, d, tn), lambda l, j: (l, 0, j)),
            pl.BlockSpec((1, 1, tn), lambda l, j: (l, 0, j)),
        ],
        out_specs=pl.BlockSpec((1, rows, tn), lambda l, j: (l, 0, j)),
        out_shape=jax.ShapeDtypeStruct((depth, rows, n), F32),
        compiler_params=pltpu.CompilerParams(
            dimension_semantics=("arbitrary", "arbitrary"), vmem_limit_bytes=VMEM_LIMIT_BYTES),
        name="ada",
    )(c_all, w_ada, b_ada.reshape(depth, 1, n))


def _swiglu_accumulate(h, wg, wu, wd):
    return _dot((_silu(_dot(h, wg)) * _dot(h, wu)).astype(BF16), wd)


def _ffn_rows_kernel(x_ref, shift_ref, scale_ref, gate_ref, gpre_ref, gpost_ref, wg_ref, wu_ref, wd_ref,
                     o_ref, *rest):
    h_ref = rest[-1]
    j = pl.program_id(0)
    n_chunks = x_ref.shape[0] // ROW_CHUNK

    @pl.when(j == 0)
    def _():
        def body(r, carry):
            sl = pl.ds(pl.multiple_of(r * ROW_CHUNK, ROW_CHUNK), ROW_CHUNK)
            xn = _rms(x_ref[sl, :], gpre_ref[...])
            h_ref[sl, :] = (xn * (1.0 + scale_ref[sl, :]) + shift_ref[sl, :]).astype(BF16)
            o_ref[sl, :] = jnp.zeros((ROW_CHUNK, o_ref.shape[1]), F32)
            return carry
        lax.fori_loop(0, n_chunks, body, 0)

    wg, wu, wd = wg_ref[...].astype(BF16), wu_ref[...].astype(BF16), wd_ref[...].astype(BF16)
    for copy_ref, w in zip(rest[:-1], (wg, wu, wd)):
        copy_ref[...] = w
    o_ref[...] += _swiglu_accumulate(h_ref[...], wg, wu, wd)

    @pl.when(j == pl.num_programs(0) - 1)
    def _():
        def body(r, carry):
            sl = pl.ds(pl.multiple_of(r * ROW_CHUNK, ROW_CHUNK), ROW_CHUNK)
            y = gate_ref[sl, :] * _rms(o_ref[sl, :], gpost_ref[...])
            o_ref[sl, :] = x_ref[sl, :] + 0.5 * y
            return carry
        lax.fori_loop(0, n_chunks, body, 0)


def _ffn_rows(x, shift, scale, gate, gpre, gpost, wg, wu, wd, *, tf, w_index=None):
    r, d = x.shape
    f = wg.shape[-1]
    row_spec = pl.BlockSpec((r, d), lambda j: (0, 0))
    vec_spec = pl.BlockSpec((1, d), lambda j: (0, 0))
    w_copy_specs = [pl.BlockSpec((d, tf), lambda j: (0, j)), pl.BlockSpec((d, tf), lambda j: (0, j)),
                    pl.BlockSpec((tf, d), lambda j: (j, 0))]
    if w_index is None:
        w_specs, copy_specs, copy_shapes = w_copy_specs, [], []
    else:
        layer, sub = w_index
        w_specs = [pl.BlockSpec((None, None, d, tf), lambda j: (layer, sub, 0, j)),
                   pl.BlockSpec((None, None, d, tf), lambda j: (layer, sub, 0, j)),
                   pl.BlockSpec((None, None, tf, d), lambda j: (layer, sub, j, 0))]
        copy_specs = w_copy_specs
        copy_shapes = [jax.ShapeDtypeStruct((d, f), BF16), jax.ShapeDtypeStruct((d, f), BF16),
                       jax.ShapeDtypeStruct((f, d), BF16)]
    out = pl.pallas_call(
        _ffn_rows_kernel,
        grid=(f // tf,),
        in_specs=[row_spec, row_spec, row_spec, row_spec, vec_spec, vec_spec] + w_specs,
        out_specs=[row_spec] + copy_specs,
        out_shape=[jax.ShapeDtypeStruct((r, d), F32)] + copy_shapes,
        scratch_shapes=[pltpu.VMEM((r, d), BF16)],
        compiler_params=pltpu.CompilerParams(
            dimension_semantics=("arbitrary",), vmem_limit_bytes=VMEM_LIMIT_BYTES),
        name="ffn_rows",
    )(x, shift, scale, gate, gpre, gpost, wg, wu, wd)
    return out[0] if w_index is None else out


def _ffn_stream_kernel(xa_ref, xc_ref, shift_ref, scale_ref, gate_ref, gpre_ref, gpost_ref, wg_ref, wu_ref,
                       wd_ref, *rest, n_tiles, n_cast, cast_every):
    cast_in, rest = rest[:n_cast], rest[n_cast:]
    o_ref, cast_out = rest[0], rest[1:1 + n_cast]
    h0_ref, h1_ref, acc0_ref, acc1_ref = rest[1 + n_cast:]
    i = pl.program_id(0)
    j = pl.program_id(1)
    rc = xa_ref.shape[0]
    row0 = pl.multiple_of(j * rc, rc)
    zeros = jnp.zeros((ROW_CHUNK, acc0_ref.shape[1]), F32)

    def pre_norm(h_ref, clear_refs):
        for s in range(rc // ROW_CHUNK):
            rows = pl.ds(row0 + s * ROW_CHUNK, ROW_CHUNK)
            xn = _rms(xa_ref[s * ROW_CHUNK:(s + 1) * ROW_CHUNK, :], gpre_ref[...])
            h_ref[rows, :] = (xn * (1.0 + scale_ref[...]) + shift_ref[...]).astype(BF16)
            for acc_ref in clear_refs:
                acc_ref[rows, :] = zeros

    def post_norm(acc_ref):
        for s in range(rc // ROW_CHUNK):
            sl = slice(s * ROW_CHUNK, (s + 1) * ROW_CHUNK)
            y = gate_ref[...] * _rms(acc_ref[pl.ds(row0 + s * ROW_CHUNK, ROW_CHUNK), :], gpost_ref[...])
            o_ref[sl, :] = xc_ref[sl, :] + 0.5 * y

    @pl.when(i == 0)
    def _():
        pre_norm(h0_ref, (acc0_ref, acc1_ref))

    for parity, (h_new, acc_new, h_mm, acc_mm) in enumerate(
            ((h0_ref, acc0_ref, h1_ref, acc1_ref), (h1_ref, acc1_ref, h0_ref, acc0_ref))):
        is_mine = lax.rem(i, 2) == parity

        @pl.when(jnp.logical_and(is_mine, jnp.logical_and(i >= 1, i <= n_tiles)))
        def _():
            post_norm(acc_new)
            pre_norm(h_new, (acc_new,))
            acc_mm[...] += _swiglu_accumulate(h_mm[...], wg_ref[...], wu_ref[...], wd_ref[...])

        @pl.when(jnp.logical_and(is_mine, i == n_tiles + 1))
        def _():
            post_norm(acc_new)

    if n_cast:
        @pl.when(jnp.logical_and(lax.rem(j, cast_every) == 0, jnp.logical_and(i >= 1, i <= n_tiles)))
        def _():
            for src_ref, dst_ref in zip(cast_in, cast_out):
                dst_ref[...] = src_ref[...].astype(BF16)


def _ffn_stream(x, shift, scale, gate, gpre, gpost, wg, wu, wd, *, tm, tf, cast_next=None):
    r, d = x.shape
    f = wg.shape[-1]
    assert shift.shape == (1, d)
    nt, nj = r // tm, f // tf
    rc = tm // nj
    assert rc % ROW_CHUNK == 0 and nt >= 2
    n_steps = nt * nj

    def w_block(i, j):
        return jnp.where(i == 0, 0, jnp.where(i == nt + 1, nj - 1, j))

    cast_args, cast_in_specs, cast_out_specs, cast_shapes = [], [], [], []
    cast_every = CAST_EVERY if nj % CAST_EVERY == 0 else 1
    n_casts = n_steps // cast_every
    if cast_next is not None:
        *w_next, layer, sub = cast_next
        for w in w_next:
            rows, cols = w.shape[2:]
            br = max(BF16_SUBLANES, rows // n_casts)
            nb = rows // br
            assert rows % br == 0 and n_casts % nb == 0

            def blk(i, j, nb=nb):
                return (jnp.clip((i - 1) * nj + j, 0, n_steps - 1) // cast_every * nb) // n_casts

            cast_args.append(w)
            cast_in_specs.append(pl.BlockSpec((None, None, br, cols), lambda i, j, blk=blk: (layer, sub, blk(i, j), 0)))
            cast_out_specs.append(pl.BlockSpec((br, cols), lambda i, j, blk=blk: (blk(i, j), 0)))
            cast_shapes.append(jax.ShapeDtypeStruct((rows, cols), BF16))

    vec_spec = pl.BlockSpec((1, d), lambda i, j: (0, 0))
    out = pl.pallas_call(
        functools.partial(_ffn_stream_kernel, n_tiles=nt, n_cast=len(cast_args), cast_every=cast_every),
        grid=(nt + 2, nj),
        in_specs=[
            pl.BlockSpec((rc, d), lambda i, j: (jnp.minimum(i, nt - 1) * nj + j, 0)),
            pl.BlockSpec((rc, d), lambda i, j: (jnp.maximum(i - 2, 0) * nj + j, 0)),
            vec_spec, vec_spec, vec_spec, vec_spec, vec_spec,
            pl.BlockSpec((d, tf), lambda i, j: (0, w_block(i, j))),
            pl.BlockSpec((d, tf), lambda i, j: (0, w_block(i, j))),
            pl.BlockSpec((tf, d), lambda i, j: (w_block(i, j), 0)),
        ] + cast_in_specs,
        out_specs=[pl.BlockSpec((rc, d), lambda i, j: (jnp.maximum(i - 2, 0) * nj + jnp.where(i >= 2, j, 0), 0))]
        + cast_out_specs,
        out_shape=[jax.ShapeDtypeStruct((r, d), F32)] + cast_shapes,
        scratch_shapes=[pltpu.VMEM((tm, d), BF16), pltpu.VMEM((tm, d), BF16),
                        pltpu.VMEM((tm, d), F32), pltpu.VMEM((tm, d), F32)],
        compiler_params=pltpu.CompilerParams(
            dimension_semantics=("arbitrary", "arbitrary"), vmem_limit_bytes=VMEM_LIMIT_BYTES),
        name="ffn_stream",
    )(x, x, shift, scale, gate, gpre, gpost, wg, wu, wd, *cast_args)
    return out[0], tuple(out[1:])


def _split3(x):
    hi = x.astype(BF16)
    r1 = x - hi.astype(F32)
    mid = r1.astype(BF16)
    lo = (r1 - mid.astype(F32)).astype(BF16)
    return hi, mid, lo


def _pad_rows(a, rows):
    if a.shape[0] == rows:
        return a
    return jnp.concatenate([a, jnp.zeros((rows - a.shape[0], a.shape[1]), a.dtype)], axis=0)


def _mixer_kernel(x_ref, shift_ref, scale_ref, gate_ref, gpre_ref, gpost_ref, win_ref, wfr_ref, wfg_ref,
                  bfg_ref, wpool_ref, pscale_ref, gnorm_ref, wout_ref, pool_prev_ref, gla_prev_ref,
                  o_ref, pool_new_ref, gla_new_ref, z_ref, ext_ref, s_ref, mix_ref, *, start, tp):
    t = pl.program_id(1)
    tm, d = x_ref.shape
    pw = ext_ref.shape[1]
    gw = pw // len(POOL_WINDOWS)
    kw = wfg_ref.shape[1]
    dk = kw // GLA_HEADS
    vw = gnorm_ref.shape[1]
    dv = vw // GLA_HEADS
    q0, k0, v0, g0 = pw, pw + kw, pw + 2 * kw, pw + 2 * kw + vw
    n_blk = tp // CHUNK

    @pl.when(t == 0)
    def _():
        ext_ref[0:HIST_ROWS, :] = pool_prev_ref[...]
        s_ref[...] = gla_prev_ref[...]

    row = lax.broadcasted_iota(jnp.int32, (tp, tp), 0)
    col = lax.broadcasted_iota(jnp.int32, (tp, tp), 1)
    causal = jnp.logical_and(row >> CHUNK_SHIFT == col >> CHUNK_SHIFT, col <= row)
    tril = jnp.where(causal, 1.0, 0.0).astype(BF16)

    h = (_rms(x_ref[...], gpre_ref[...]) * (1.0 + scale_ref[...]) + shift_ref[...]).astype(BF16)
    fr = _dot(h, wfr_ref[...])
    z_ref[:, 0:q0] = _dot(h, win_ref[:, 0:q0])
    logf = _log_sigmoid(_dot(fr.astype(BF16), wfg_ref[...]) + bfg_ref[...]) / GLA_TAU
    z_ref[:, q0:v0] = _dot(h, win_ref[:, q0:v0])
    logf = _pad_rows(logf, tp)
    if tp != tm:
        logf = jnp.where(lax.broadcasted_iota(jnp.int32, (tp, 1), 0) < tm, logf, 0.0)
    hi, mid, lo = _split3(logf)
    b = _dot(tril, hi) + _dot(tril, mid) + _dot(tril, lo)
    z_ref[:, v0:] = _dot(h, win_ref[:, v0:])

    ext_ref[HIST_ROWS:HIST_ROWS + tm, :] = z_ref[:, 0:pw]
    pos = start + t * tm + lax.broadcasted_iota(jnp.int32, (tm, 1), 0)
    for gi, w in enumerate(POOL_WINDOWS):
        cols = slice(gi * gw, (gi + 1) * gw)
        u = ext_ref[HIST_ROWS:HIST_ROWS + tm, cols]
        wsum = u
        for back in range(1, w):
            wsum = wsum + ext_ref[HIST_ROWS - back:HIST_ROWS - back + tm, cols]
        cnt = jnp.minimum(pos + 1, w).astype(F32)
        m = (wsum / cnt - u).astype(BF16)
        y = _dot(m, wpool_ref[gi]) * pscale_ref[:, cols]
        mix_ref[:, cols] = y.astype(BF16)
    hist = ext_ref[tm:tm + HIST_ROWS, :]
    ext_ref[0:HIST_ROWS, :] = hist
    pool_new_ref[...] = hist

    heads = range(GLA_HEADS)
    blocks = [slice(c * CHUNK, (c + 1) * CHUNK) for c in range(n_blk)]
    k, v, bh, qe, att = [], [], [], [], []
    for hh in heads:
        q = _pad_rows(z_ref[:, q0 + hh * dk:q0 + (hh + 1) * dk], tp)
        k.append(_pad_rows(z_ref[:, k0 + hh * dk:k0 + (hh + 1) * dk], tp))
        v.append(_pad_rows(z_ref[:, v0 + hh * dv:v0 + (hh + 1) * dv], tp).astype(BF16))
        bh.append(b[:, hh * dk:(hh + 1) * dk])
        qe.append((q * jnp.exp(bh[hh]) * (dk ** -0.5)).astype(BF16))
        ke = (k[hh] * jnp.exp(-bh[hh])).astype(BF16)
        att.append(jnp.where(causal, _dot_nt(qe[hh], ke), 0.0).astype(BF16))
    o = [_dot(att[hh], v[hh]) for hh in heads]
    kv, decay = [], []
    for hh in heads:
        kv.append([])
        decay.append([])
        for rows in blocks:
            bt = bh[hh][rows].T
            b_last = bt[:, CHUNK - 1:CHUNK]
            kd_t = (k[hh][rows].T * jnp.exp(b_last - bt)).astype(BF16)
            kv[hh].append(_dot(kd_t, v[hh][rows]))
            decay[hh].append(jnp.exp(b_last))
    state = [s_ref[hh] for hh in heads]
    o_state = [[] for _ in heads]
    for c, rows in enumerate(blocks):
        for hh in heads:
            o_state[hh].append(_dot(qe[hh][rows], state[hh].astype(BF16)))
            state[hh] = decay[hh][c] * state[hh] + kv[hh][c]
    for hh in heads:
        s_ref[hh] = state[hh]
        gla_new_ref[hh] = state[hh]
        o_h = o[hh] + (jnp.concatenate(o_state[hh], axis=0) if n_blk > 1 else o_state[hh][0])
        o_h = _rms(o_h[0:tm], gnorm_ref[:, hh * dv:(hh + 1) * dv])
        og = z_ref[:, g0 + hh * dv:g0 + (hh + 1) * dv]
        mix_ref[:, pw + hh * dv:pw + (hh + 1) * dv] = (o_h * _silu(og)).astype(BF16)

    y = _dot(mix_ref[...], wout_ref[...])
    o_ref[...] = x_ref[...] + gate_ref[...] * _rms(y, gpost_ref[...])


def _mixer(x, shift, scale, gate, gpre, gpost, w_in, w_fr, w_fg, b_fg, w_pool, pscale, gnorm, w_out,
           pool_prev, gla_prev, layer, *, n_seq, tm, start):
    r, d = x.shape
    n_tiles = r // n_seq // tm
    tp = max(tm, CHUNK)
    pw = pool_prev.shape[-1]
    hds, dk, dv = gla_prev.shape[1:]
    zw = pw + 2 * hds * (dk + dv)
    const = dict(pipeline_mode=pl.Buffered(1))

    def seq_spec(shape):
        return pl.BlockSpec((None,) + shape, lambda s, t: (s,) + (0,) * len(shape))

    def layer_spec(shape):
        return pl.BlockSpec((None,) + shape, lambda s, t: (layer,) + (0,) * len(shape), **const)

    row_spec = pl.BlockSpec((tm, d), lambda s, t: (s * n_tiles + t, 0))
    return pl.pallas_call(
        functools.partial(_mixer_kernel, start=start, tp=tp),
        grid=(n_seq, n_tiles),
        in_specs=[
            row_spec,
            seq_spec((1, d)), seq_spec((1, d)), seq_spec((1, d)),
            pl.BlockSpec((1, d), lambda s, t: (0, 0)), pl.BlockSpec((1, d), lambda s, t: (0, 0)),
            layer_spec((d, zw)), layer_spec((d, GLA_RANK_PAD)), layer_spec((GLA_RANK_PAD, hds * dk)),
            layer_spec((1, hds * dk)), layer_spec(w_pool.shape[1:]), layer_spec((1, pw)),
            layer_spec((1, hds * dv)), layer_spec((pw + hds * dv, d)),
            seq_spec((HIST_ROWS, pw)), seq_spec((hds, dk, dv)),
        ],
        out_specs=[row_spec, seq_spec((HIST_ROWS, pw)), seq_spec((hds, dk, dv))],
        out_shape=[
            jax.ShapeDtypeStruct((r, d), F32),
            jax.ShapeDtypeStruct((n_seq, HIST_ROWS, pw), F32),
            jax.ShapeDtypeStruct((n_seq, hds, dk, dv), F32),
        ],
        scratch_shapes=[
            pltpu.VMEM((tm, zw), F32),
            pltpu.VMEM((HIST_ROWS + tm, pw), F32),
            pltpu.VMEM((hds, dk, dv), F32),
            pltpu.VMEM((tm, pw + hds * dv), BF16),
        ],
        compiler_params=pltpu.CompilerParams(
            dimension_semantics=("arbitrary", "arbitrary"), vmem_limit_bytes=VMEM_LIMIT_BYTES),
        name="mixer",
    )(x, shift, scale, gate, gpre, gpost, w_in, w_fr, w_fg, b_fg, w_pool, pscale, gnorm, w_out,
      pool_prev, gla_prev)


PROMPT_ROW_TILE = 512
PROMPT_FF_TILE = 1024
SAMPLE_FF_TILE = 512
MIXER_ROW_TILE = 256


def kernel(x_prompt, x_sample, state_pool, state_gla, c_prompt, c_sample, w_ada, b_ada, norm_pre, norm_post,
           w_ffn_gate, w_ffn_up, w_ffn_down, w_in, w_forget, b_forget, w_pool, pool_scale, gla_norm, w_out):
    bp, seq, d = x_prompt.shape
    bs, dec_seq, _ = x_sample.shape
    depth = w_ada.shape[0]
    pw = state_pool.shape[-1]
    hds, dk, dv = state_gla.shape[2:]
    kw, vw = hds * dk, hds * dv
    zw = pw + 2 * kw + 2 * vw
    rank = w_forget.shape[1]
    d_ff = w_ffn_gate.shape[-1]

    n_c = bp + bs
    c_rows = -(-n_c // BF16_SUBLANES) * BF16_SUBLANES
    c_all = jnp.concatenate([c_prompt, c_sample, jnp.zeros((c_rows - n_c, d), F32)], axis=0)
    mod = _ada(c_all, w_ada, b_ada).reshape(depth, c_rows, N_SUB, 3, d)

    w_in_b = w_in.astype(BF16)
    w_fr = jnp.pad(w_in[:, :, zw:], ((0, 0), (0, 0), (0, GLA_RANK_PAD - rank))).astype(BF16)
    w_fg = jnp.pad(w_forget, ((0, 0), (0, GLA_RANK_PAD - rank), (0, 0))).astype(BF16)
    w_pool_b, w_out_b = w_pool.astype(BF16), w_out.astype(BF16)

    xp = x_prompt.reshape(bp * seq, d)
    xs = x_sample.reshape(bs * dec_seq, d)
    pool0 = jnp.zeros((bp, HIST_ROWS, pw), F32)
    gla0 = jnp.zeros((bp, hds, dk, dv), F32)
    pool_s_in = jnp.pad(state_pool, ((0, 0), (0, 0), (1, 0), (0, 0)))
    tm_p = min(PROMPT_ROW_TILE, bp * seq // 2)
    tf_p, tf_s = min(PROMPT_FF_TILE, d_ff), min(SAMPLE_FF_TILE, d_ff)
    tm_mix = min(seq, MIXER_ROW_TILE)

    outs = ([], [], [], [])
    w_b = None
    for l in range(depth):
        gpre, gpost = norm_pre[l][:, None, :], norm_post[l][:, None, :]
        mp = mod[l, :bp]
        ms = mod[l, bp:n_c]
        ms_rows = jnp.repeat(ms, dec_seq, axis=0)
        lw = (w_in_b, w_fr, w_fg, b_forget[:, None, :], w_pool_b, pool_scale[:, None, :],
              gla_norm[:, None, :], w_out_b)

        def ffn(xp, xs, w_b, sub, i):
            mods = (ms_rows[:, i, 0], ms_rows[:, i, 1], ms_rows[:, i, 2], gpre[i], gpost[i])
            if w_b is None:
                xs, *w_b = _ffn_rows(xs, *mods, w_ffn_gate, w_ffn_up, w_ffn_down, tf=tf_s, w_index=(l, sub))
            else:
                xs = _ffn_rows(xs, *mods, *w_b, tf=tf_p)
            nxt = (l, 1) if sub == 0 else (l + 1, 0)
            cast_next = (w_ffn_gate, w_ffn_up, w_ffn_down) + nxt if nxt[0] < depth else None
            xp, w_next = _ffn_stream(xp, mp[:, i, 0], mp[:, i, 1], mp[:, i, 2], gpre[i], gpost[i], *w_b,
                                     tm=tm_p, tf=tf_p, cast_next=cast_next)
            return xp, xs, (w_next or None)

        xp, xs, w_b = ffn(xp, xs, w_b, 0, 0)
        xp, pool_p, gla_p = _mixer(xp, mp[:, 1, 0][:, None], mp[:, 1, 1][:, None], mp[:, 1, 2][:, None],
                                   gpre[1], gpost[1], *lw, pool0, gla0, l, n_seq=bp, tm=tm_mix, start=0)
        xs, pool_s, gla_s = _mixer(xs, ms[:, 1, 0][:, None], ms[:, 1, 1][:, None], ms[:, 1, 2][:, None],
                                   gpre[1], gpost[1], *lw, pool_s_in[l], state_gla[l], l, n_seq=bs, tm=dec_seq,
                                   start=PAST_LEN)
        xp, xs, w_b = ffn(xp, xs, w_b, 1, 2)
        for acc, val in zip(outs, (pool_p[:, 1:], gla_p, pool_s[:, 1:], gla_s)):
            acc.append(val)

    return (xp.reshape(bp, seq, d), xs.reshape(bs, dec_seq, d),
            jnp.stack(outs[0]), jnp.stack(outs[1]), jnp.stack(outs[2]), jnp.stack(outs[3]))
```

```python
import functools

import jax
import jax.numpy as jnp
from jax import lax
from jax.experimental import pallas as pl
from jax.experimental.pallas import tpu as pltpu

F32 = jnp.float32
BF16 = jnp.bfloat16

EPS = 1e-6
N_SUB = 3
CHUNK = 64
CHUNK_SHIFT = CHUNK.bit_length() - 1
assert CHUNK == 1 << CHUNK_SHIFT
POOL_WINDOWS = (2, 4, 8, 16)
POOL_HIST = max(POOL_WINDOWS) - 1
HIST_ROWS = POOL_HIST + 1
GLA_HEADS = 4
GLA_TAU = 16.0
GLA_RANK_PAD = 128
PAST_LEN = 4096

ROW_CHUNK = 16
BF16_SUBLANES = 16
CAST_EVERY = 4
ROW_BLOCK_STEPS = 2
VMEM_LIMIT_BYTES = 56 * 1024 * 1024


def _silu(x):
    return x * jax.nn.sigmoid(x)


def _log_sigmoid(x):
    return jnp.minimum(x, 0.0) - jnp.log1p(jnp.exp(-jnp.abs(x)))


def _rms(x, g):
    return x * lax.rsqrt(jnp.mean(x * x, axis=-1, keepdims=True) + EPS) * g


def _dot(a, b):
    return jnp.dot(a, b, preferred_element_type=F32)


def _dot_nt(a, b):
    return lax.dot_general(a, b, (((1,), (1,)), ((), ())), preferred_element_type=F32)


def _ada_kernel(c_ref, w_ref, b_ref, o_ref):
    c = c_ref[...]
    o_ref[0] = _dot(_silu(c).astype(BF16), w_ref[0].astype(BF16)) + b_ref[0]


def _ada(c_all, w_ada, b_ada, tn=1024):
    depth, d, n = w_ada.shape
    rows = c_all.shape[0]
    return pl.pallas_call(
        _ada_kernel,
        grid=(depth, n // tn),
        in_specs=[
            pl.BlockSpec((rows, d), lambda l, j: (0, 0)),
            pl.BlockSpec((1, d, tn), lambda l, j: (l, 0, j)),
            pl.BlockSpec((1, 1, tn), lambda l, j: (l, 0, j)),
        ],
        out_specs=pl.BlockSpec((1, rows, tn), lambda l, j: (l, 0, j)),
        out_shape=jax.ShapeDtypeStruct((depth, rows, n), F32),
        compiler_params=pltpu.CompilerParams(
            dimension_semantics=("arbitrary", "arbitrary"), vmem_limit_bytes=VMEM_LIMIT_BYTES),
        name="ada",
    )(c_all, w_ada, b_ada.reshape(depth, 1, n))


def _swiglu_accumulate(h, wg, wu, wd):
    return _dot((_silu(_dot(h, wg)) * _dot(h, wu)).astype(BF16), wd)


def _ffn_rows_kernel(x_ref, shift_ref, scale_ref, gate_ref, gpre_ref, gpost_ref, wg_ref, wu_ref, wd_ref,
                     o_ref, *rest):
    h_ref = rest[-1]
    j = pl.program_id(0)
    n_chunks = x_ref.shape[0] // ROW_CHUNK

    @pl.when(j == 0)
    def _():
        def body(r, carry):
            sl = pl.ds(pl.multiple_of(r * ROW_CHUNK, ROW_CHUNK), ROW_CHUNK)
            xn = _rms(x_ref[sl, :], gpre_ref[...])
            h_ref[sl, :] = (xn * (1.0 + scale_ref[sl, :]) + shift_ref[sl, :]).astype(BF16)
            o_ref[sl, :] = jnp.zeros((ROW_CHUNK, o_ref.shape[1]), F32)
            return carry
        lax.fori_loop(0, n_chunks, body, 0)

    wg, wu, wd = wg_ref[...].astype(BF16), wu_ref[...].astype(BF16), wd_ref[...].astype(BF16)
    for copy_ref, w in zip(rest[:-1], (wg, wu, wd)):
        copy_ref[...] = w
    o_ref[...] += _swiglu_accumulate(h_ref[...], wg, wu, wd)

    @pl.when(j == pl.num_programs(0) - 1)
    def _():
        def body(r, carry):
            sl = pl.ds(pl.multiple_of(r * ROW_CHUNK, ROW_CHUNK), ROW_CHUNK)
            y = gate_ref[sl, :] * _rms(o_ref[sl, :], gpost_ref[...])
            o_ref[sl, :] = x_ref[sl, :] + 0.5 * y
            return carry
        lax.fori_loop(0, n_chunks, body, 0)


def _ffn_rows(x, shift, scale, gate, gpre, gpost, wg, wu, wd, *, tf, w_index=None):
    r, d = x.shape
    f = wg.shape[-1]
    row_spec = pl.BlockSpec((r, d), lambda j: (0, 0))
    vec_spec = pl.BlockSpec((1, d), lambda j: (0, 0))
    w_copy_specs = [pl.BlockSpec((d, tf), lambda j: (0, j)), pl.BlockSpec((d, tf), lambda j: (0, j)),
                    pl.BlockSpec((tf, d), lambda j: (j, 0))]
    if w_index is None:
        w_specs, copy_specs, copy_shapes = w_copy_specs, [], []
    else:
        layer, sub = w_index
        w_specs = [pl.BlockSpec((None, None, d, tf), lambda j: (layer, sub, 0, j)),
                   pl.BlockSpec((None, None, d, tf), lambda j: (layer, sub, 0, j)),
                   pl.BlockSpec((None, None, tf, d), lambda j: (layer, sub, j, 0))]
        copy_specs = w_copy_specs
        copy_shapes = [jax.ShapeDtypeStruct((d, f), BF16), jax.ShapeDtypeStruct((d, f), BF16),
                       jax.ShapeDtypeStruct((f, d), BF16)]
    out = pl.pallas_call(
        _ffn_rows_kernel,
        grid=(f // tf,),
        in_specs=[row_spec, row_spec, row_spec, row_spec, vec_spec, vec_spec] + w_specs,
        out_specs=[row_spec] + copy_specs,
        out_shape=[jax.ShapeDtypeStruct((r, d), F32)] + copy_shapes,
        scratch_shapes=[pltpu.VMEM((r, d), BF16)],
        compiler_params=pltpu.CompilerParams(
            dimension_semantics=("arbitrary",), vmem_limit_bytes=VMEM_LIMIT_BYTES),
        name="ffn_rows",
    )(x, shift, scale, gate, gpre, gpost, wg, wu, wd)
    return out[0] if w_index is None else out


def _ffn_stream_kernel(xa_ref, xc_ref, shift_ref, scale_ref, gate_ref, gpre_ref, gpost_ref, wg_ref, wu_ref,
                       wd_ref, *rest, n_tiles, n_cast, cast_every, steps_per_block):
    cast_in, rest = rest[:n_cast], rest[n_cast:]
    o_ref, cast_out = rest[0], rest[1:1 + n_cast]
    h0_ref, h1_ref, acc0_ref, acc1_ref = rest[1 + n_cast:]
    i = pl.program_id(0)
    j = pl.program_id(1)
    rc = xa_ref.shape[0] // steps_per_block
    row0 = pl.multiple_of(j * rc, rc)
    blk0 = pl.multiple_of(lax.rem(j, steps_per_block) * rc, rc)
    zeros = jnp.zeros((ROW_CHUNK, acc0_ref.shape[1]), F32)

    def pre_norm(h_ref, clear_refs):
        for s in range(rc // ROW_CHUNK):
            rows = pl.ds(row0 + s * ROW_CHUNK, ROW_CHUNK)
            xn = _rms(xa_ref[pl.ds(blk0 + s * ROW_CHUNK, ROW_CHUNK), :], gpre_ref[...])
            h_ref[rows, :] = (xn * (1.0 + scale_ref[...]) + shift_ref[...]).astype(BF16)
            for acc_ref in clear_refs:
                acc_ref[rows, :] = zeros

    def post_norm(acc_ref):
        for s in range(rc // ROW_CHUNK):
            sl = pl.ds(blk0 + s * ROW_CHUNK, ROW_CHUNK)
            y = gate_ref[...] * _rms(acc_ref[pl.ds(row0 + s * ROW_CHUNK, ROW_CHUNK), :], gpost_ref[...])
            o_ref[sl, :] = xc_ref[sl, :] + 0.5 * y

    @pl.when(i == 0)
    def _():
        pre_norm(h0_ref, (acc0_ref, acc1_ref))

    for parity, (h_new, acc_new, h_mm, acc_mm) in enumerate(
            ((h0_ref, acc0_ref, h1_ref, acc1_ref), (h1_ref, acc1_ref, h0_ref, acc0_ref))):
        is_mine = lax.rem(i, 2) == parity

        @pl.when(jnp.logical_and(is_mine, jnp.logical_and(i >= 1, i <= n_tiles)))
        def _():
            post_norm(acc_new)
            pre_norm(h_new, (acc_new,))
            acc_mm[...] += _swiglu_accumulate(h_mm[...], wg_ref[...], wu_ref[...], wd_ref[...])

        @pl.when(jnp.logical_and(is_mine, i == n_tiles + 1))
        def _():
            post_norm(acc_new)

    if n_cast:
        @pl.when(jnp.logical_and(lax.rem(j, cast_every) == 0, jnp.logical_and(i >= 1, i <= n_tiles)))
        def _():
            for src_ref, dst_ref in zip(cast_in, cast_out):
                dst_ref[...] = src_ref[...].astype(BF16)


def _ffn_stream(x, shift, scale, gate, gpre, gpost, wg, wu, wd, *, tm, tf, cast_next=None):
    r, d = x.shape
    f = wg.shape[-1]
    assert shift.shape == (1, d)
    nt, nj = r // tm, f // tf
    rc = tm // nj
    assert rc % ROW_CHUNK == 0 and nt >= 2
    n_steps = nt * nj

    def w_block(i, j):
        return jnp.where(i == 0, 0, jnp.where(i == nt + 1, nj - 1, j))

    cast_args, cast_in_specs, cast_out_specs, cast_shapes = [], [], [], []
    cast_every = CAST_EVERY if nj % CAST_EVERY == 0 else 1
    n_casts = n_steps // cast_every
    if cast_next is not None:
        *w_next, layer, sub = cast_next
        for w in w_next:
            rows, cols = w.shape[2:]
            br = max(BF16_SUBLANES, rows // n_casts)
            nb = rows // br
            assert rows % br == 0 and n_casts % nb == 0

            def blk(i, j, nb=nb):
                return (jnp.clip((i - 1) * nj + j, 0, n_steps - 1) // cast_every * nb) // n_casts

            cast_args.append(w)
            cast_in_specs.append(pl.BlockSpec((None, None, br, cols), lambda i, j, blk=blk: (layer, sub, blk(i, j), 0)))
            cast_out_specs.append(pl.BlockSpec((br, cols), lambda i, j, blk=blk: (blk(i, j), 0)))
            cast_shapes.append(jax.ShapeDtypeStruct((rows, cols), BF16))

    vec_spec = pl.BlockSpec((1, d), lambda i, j: (0, 0))
    spb = ROW_BLOCK_STEPS if nj % ROW_BLOCK_STEPS == 0 else 1
    rb = rc * spb
    out = pl.pallas_call(
        functools.partial(_ffn_stream_kernel, n_tiles=nt, n_cast=len(cast_args), cast_every=cast_every,
                          steps_per_block=spb),
        grid=(nt + 2, nj),
        in_specs=[
            pl.BlockSpec((rb, d), lambda i, j: ((jnp.minimum(i, nt - 1) * nj + j) // spb, 0)),
            pl.BlockSpec((rb, d), lambda i, j: ((jnp.maximum(i - 2, 0) * nj + j) // spb, 0)),
            vec_spec, vec_spec, vec_spec, vec_spec, vec_spec,
            pl.BlockSpec((d, tf), lambda i, j: (0, w_block(i, j))),
            pl.BlockSpec((d, tf), lambda i, j: (0, w_block(i, j))),
            pl.BlockSpec((tf, d), lambda i, j: (w_block(i, j), 0)),
        ] + cast_in_specs,
        out_specs=[pl.BlockSpec(
            (rb, d), lambda i, j: ((jnp.maximum(i - 2, 0) * nj + jnp.where(i >= 2, j, 0)) // spb, 0))]
        + cast_out_specs,
        out_shape=[jax.ShapeDtypeStruct((r, d), F32)] + cast_shapes,
        scratch_shapes=[pltpu.VMEM((tm, d), BF16), pltpu.VMEM((tm, d), BF16),
                        pltpu.VMEM((tm, d), F32), pltpu.VMEM((tm, d), F32)],
        compiler_params=pltpu.CompilerParams(
            dimension_semantics=("arbitrary", "arbitrary"), vmem_limit_bytes=VMEM_LIMIT_BYTES),
        name="ffn_stream",
    )(x, x, shift, scale, gate, gpre, gpost, wg, wu, wd, *cast_args)
    return out[0], tuple(out[1:])


def _split3(x):
    hi = x.astype(BF16)
    r1 = x - hi.astype(F32)
    mid = r1.astype(BF16)
    lo = (r1 - mid.astype(F32)).astype(BF16)
    return hi, mid, lo


def _pad_rows(a, rows):
    if a.shape[0] == rows:
        return a
    return jnp.concatenate([a, jnp.zeros((rows - a.shape[0], a.shape[1]), a.dtype)], axis=0)


def _mixer_kernel(x_ref, shift_ref, scale_ref, gate_ref, gpre_ref, gpost_ref, win_ref, wfr_ref, wfg_ref,
                  bfg_ref, wpool_ref, pscale_ref, gnorm_ref, wout_ref, pool_prev_ref, gla_prev_ref,
                  o_ref, pool_new_ref, gla_new_ref, z_ref, ext_ref, s_ref, mix_ref, *, start, tp):
    t = pl.program_id(1)
    tm, d = x_ref.shape
    pw = ext_ref.shape[1]
    gw = pw // len(POOL_WINDOWS)
    kw = wfg_ref.shape[1]
    dk = kw // GLA_HEADS
    vw = gnorm_ref.shape[1]
    dv = vw // GLA_HEADS
    q0, k0, v0, g0 = pw, pw + kw, pw + 2 * kw, pw + 2 * kw + vw
    n_blk = tp // CHUNK

    @pl.when(t == 0)
    def _():
        ext_ref[0:HIST_ROWS, :] = pool_prev_ref[...]
        s_ref[...] = gla_prev_ref[...]

    row = lax.broadcasted_iota(jnp.int32, (tp, tp), 0)
    col = lax.broadcasted_iota(jnp.int32, (tp, tp), 1)
    causal = jnp.logical_and(row >> CHUNK_SHIFT == col >> CHUNK_SHIFT, col <= row)
    tril = jnp.where(causal, 1.0, 0.0).astype(BF16)

    h = (_rms(x_ref[...], gpre_ref[...]) * (1.0 + scale_ref[...]) + shift_ref[...]).astype(BF16)
    fr = _dot(h, wfr_ref[...])
    z_ref[:, 0:q0] = _dot(h, win_ref[:, 0:q0])
    logf = _log_sigmoid(_dot(fr.astype(BF16), wfg_ref[...]) + bfg_ref[...]) / GLA_TAU
    z_ref[:, q0:v0] = _dot(h, win_ref[:, q0:v0])
    logf = _pad_rows(logf, tp)
    if tp != tm:
        logf = jnp.where(lax.broadcasted_iota(jnp.int32, (tp, 1), 0) < tm, logf, 0.0)
    hi, mid, lo = _split3(logf)
    b = _dot(tril, hi) + _dot(tril, mid) + _dot(tril, lo)
    z_ref[:, v0:] = _dot(h, win_ref[:, v0:])

    ext_ref[HIST_ROWS:HIST_ROWS + tm, :] = z_ref[:, 0:pw]
    pos = start + t * tm + lax.broadcasted_iota(jnp.int32, (tm, 1), 0)
    for gi, w in enumerate(POOL_WINDOWS):
        cols = slice(gi * gw, (gi + 1) * gw)
        u = ext_ref[HIST_ROWS:HIST_ROWS + tm, cols]
        wsum = u
        for back in range(1, w):
            wsum = wsum + ext_ref[HIST_ROWS - back:HIST_ROWS - back + tm, cols]
        cnt = jnp.minimum(pos + 1, w).astype(F32)
        m = (wsum / cnt - u).astype(BF16)
        y = _dot(m, wpool_ref[gi]) * pscale_ref[:, cols]
        mix_ref[:, cols] = y.astype(BF16)
    hist = ext_ref[tm:tm + HIST_ROWS, :]
    ext_ref[0:HIST_ROWS, :] = hist
    pool_new_ref[...] = hist

    heads = range(GLA_HEADS)
    blocks = [slice(c * CHUNK, (c + 1) * CHUNK) for c in range(n_blk)]
    k, v, bh, qe, att = [], [], [], [], []
    for hh in heads:
        q = _pad_rows(z_ref[:, q0 + hh * dk:q0 + (hh + 1) * dk], tp)
        k.append(_pad_rows(z_ref[:, k0 + hh * dk:k0 + (hh + 1) * dk], tp))
        v.append(_pad_rows(z_ref[:, v0 + hh * dv:v0 + (hh + 1) * dv], tp).astype(BF16))
        bh.append(b[:, hh * dk:(hh + 1) * dk])
        qe.append((q * jnp.exp(bh[hh]) * (dk ** -0.5)).astype(BF16))
        ke = (k[hh] * jnp.exp(-bh[hh])).astype(BF16)
        att.append(jnp.where(causal, _dot_nt(qe[hh], ke), 0.0).astype(BF16))
    o = [_dot(att[hh], v[hh]) for hh in heads]
    kv, decay = [], []
    for hh in heads:
        kv.append([])
        decay.append([])
        for rows in blocks:
            bt = bh[hh][rows].T
            b_last = bt[:, CHUNK - 1:CHUNK]
            kd_t = (k[hh][rows].T * jnp.exp(b_last - bt)).astype(BF16)
            kv[hh].append(_dot(kd_t, v[hh][rows]))
            decay[hh].append(jnp.exp(b_last))
    state = [s_ref[hh] for hh in heads]
    o_state = [[] for _ in heads]
    for c, rows in enumerate(blocks):
        for hh in heads:
            o_state[hh].append(_dot(qe[hh][rows], state[hh].astype(BF16)))
            state[hh] = decay[hh][c] * state[hh] + kv[hh][c]
    for hh in heads:
        s_ref[hh] = state[hh]
        gla_new_ref[hh] = state[hh]
        o_h = o[hh] + (jnp.concatenate(o_state[hh], axis=0) if n_blk > 1 else o_state[hh][0])
        o_h = _rms(o_h[0:tm], gnorm_ref[:, hh * dv:(hh + 1) * dv])
        og = z_ref[:, g0 + hh * dv:g0 + (hh + 1) * dv]
        mix_ref[:, pw + hh * dv:pw + (hh + 1) * dv] = (o_h * _silu(og)).astype(BF16)

    y = _dot(mix_ref[...], wout_ref[...])
    o_ref[...] = x_ref[...] + gate_ref[...] * _rms(y, gpost_ref[...])


def _mixer(x, shift, scale, gate, gpre, gpost, w_in, w_fr, w_fg, b_fg, w_pool, pscale, gnorm, w_out,
           pool_prev, gla_prev, layer, *, n_seq, tm, start):
    r, d = x.shape
    n_tiles = r // n_seq // tm
    tp = max(tm, CHUNK)
    pw = pool_prev.shape[-1]
    hds, dk, dv = gla_prev.shape[1:]
    zw = pw + 2 * hds * (dk + dv)
    const = dict(pipeline_mode=pl.Buffered(1))

    def seq_spec(shape):
        return pl.BlockSpec((None,) + shape, lambda s, t: (s,) + (0,) * len(shape))

    def layer_spec(shape):
        return pl.BlockSpec((None,) + shape, lambda s, t: (layer,) + (0,) * len(shape), **const)

    row_spec = pl.BlockSpec((tm, d), lambda s, t: (s * n_tiles + t, 0))
    return pl.pallas_call(
        functools.partial(_mixer_kernel, start=start, tp=tp),
        grid=(n_seq, n_tiles),
        in_specs=[
            row_spec,
            seq_spec((1, d)), seq_spec((1, d)), seq_spec((1, d)),
            pl.BlockSpec((1, d), lambda s, t: (0, 0)), pl.BlockSpec((1, d), lambda s, t: (0, 0)),
            layer_spec((d, zw)), layer_spec((d, GLA_RANK_PAD)), layer_spec((GLA_RANK_PAD, hds * dk)),
            layer_spec((1, hds * dk)), layer_spec(w_pool.shape[1:]), layer_spec((1, pw)),
            layer_spec((1, hds * dv)), layer_spec((pw + hds * dv, d)),
            seq_spec((HIST_ROWS, pw)), seq_spec((hds, dk, dv)),
        ],
        out_specs=[row_spec, seq_spec((HIST_ROWS, pw)), seq_spec((hds, dk, dv))],
        out_shape=[
            jax.ShapeDtypeStruct((r, d), F32),
            jax.ShapeDtypeStruct((n_seq, HIST_ROWS, pw), F32),
            jax.ShapeDtypeStruct((n_seq, hds, dk, dv), F32),
        ],
        scratch_shapes=[
            pltpu.VMEM((tm, zw), F32),
            pltpu.VMEM((HIST_ROWS + tm, pw), F32),
            pltpu.VMEM((hds, dk, dv), F32),
            pltpu.VMEM((tm, pw + hds * dv), BF16),
        ],
        compiler_params=pltpu.CompilerParams(
            dimension_semantics=("arbitrary", "arbitrary"), vmem_limit_bytes=VMEM_LIMIT_BYTES),
        name="mixer",
    )(x, shift, scale, gate, gpre, gpost, w_in, w_fr, w_fg, b_fg, w_pool, pscale, gnorm, w_out,
      pool_prev, gla_prev)


PROMPT_ROW_TILE = 512
PROMPT_FF_TILE = 1024
SAMPLE_FF_TILE = 512
MIXER_ROW_TILE = 256


def kernel(x_prompt, x_sample, state_pool, state_gla, c_prompt, c_sample, w_ada, b_ada, norm_pre, norm_post,
           w_ffn_gate, w_ffn_up, w_ffn_down, w_in, w_forget, b_forget, w_pool, pool_scale, gla_norm, w_out):
    bp, seq, d = x_prompt.shape
    bs, dec_seq, _ = x_sample.shape
    depth = w_ada.shape[0]
    pw = state_pool.shape[-1]
    hds, dk, dv = state_gla.shape[2:]
    kw, vw = hds * dk, hds * dv
    zw = pw + 2 * kw + 2 * vw
    rank = w_forget.shape[1]
    d_ff = w_ffn_gate.shape[-1]

    n_c = bp + bs
    c_rows = -(-n_c // 16) * 16
    c_all = jnp.concatenate([c_prompt, c_sample, jnp.zeros((c_rows - n_c, d), F32)], axis=0)
    mod = _ada(c_all, w_ada, b_ada).reshape(depth, c_rows, N_SUB, 3, d)

    w_in_b = w_in.astype(BF16)
    w_fr = jnp.pad(w_in[:, :, zw:], ((0, 0), (0, 0), (0, GLA_RANK_PAD - rank))).astype(BF16)
    w_fg = jnp.pad(w_forget, ((0, 0), (0, GLA_RANK_PAD - rank), (0, 0))).astype(BF16)
    w_pool_b, w_out_b = w_pool.astype(BF16), w_out.astype(BF16)

    xp = x_prompt.reshape(bp * seq, d)
    xs = x_sample.reshape(bs * dec_seq, d)
    pool0 = jnp.zeros((bp, HIST_ROWS, pw), F32)
    gla0 = jnp.zeros((bp, hds, dk, dv), F32)
    pool_s_in = jnp.pad(state_pool, ((0, 0), (0, 0), (1, 0), (0, 0)))
    tm_p = min(PROMPT_ROW_TILE, bp * seq // 2)
    tf_p, tf_s = min(PROMPT_FF_TILE, d_ff), min(SAMPLE_FF_TILE, d_ff)
    tm_mix = min(seq, MIXER_ROW_TILE)

    outs = ([], [], [], [])
    w_b = None
    for l in range(depth):
        gpre, gpost = norm_pre[l][:, None, :], norm_post[l][:, None, :]
        mp = mod[l, :bp]
        ms = mod[l, bp:n_c]
        ms_rows = jnp.repeat(ms, dec_seq, axis=0)
        lw = (w_in_b, w_fr, w_fg, b_forget[:, None, :], w_pool_b, pool_scale[:, None, :],
              gla_norm[:, None, :], w_out_b)

        def ffn(xp, xs, w_b, sub, i):
            mods = (ms_rows[:, i, 0], ms_rows[:, i, 1], ms_rows[:, i, 2], gpre[i], gpost[i])
            if w_b is None:
                xs, *w_b = _ffn_rows(xs, *mods, w_ffn_gate, w_ffn_up, w_ffn_down, tf=tf_s, w_index=(l, sub))
            else:
                xs = _ffn_rows(xs, *mods, *w_b, tf=tf_p)
            nxt = (l, 1) if sub == 0 else (l + 1, 0)
            cast_next = (w_ffn_gate, w_ffn_up, w_ffn_down) + nxt if nxt[0] < depth else None
            xp, w_next = _ffn_stream(xp, mp[:, i, 0], mp[:, i, 1], mp[:, i, 2], gpre[i], gpost[i], *w_b,
                                     tm=tm_p, tf=tf_p, cast_next=cast_next)
            return xp, xs, (w_next or None)

        xp, xs, w_b = ffn(xp, xs, w_b, 0, 0)
        xp, pool_p, gla_p = _mixer(xp, mp[:, 1, 0][:, None], mp[:, 1, 1][:, None], mp[:, 1, 2][:, None],
                                   gpre[1], gpost[1], *lw, pool0, gla0, l, n_seq=bp, tm=tm_mix, start=0)
        xs, pool_s, gla_s = _mixer(xs, ms[:, 1, 0][:, None], ms[:, 1, 1][:, None], ms[:, 1, 2][:, None],
                                   gpre[1], gpost[1], *lw, pool_s_in[l], state_gla[l], l, n_seq=bs, tm=dec_seq,
                                   start=PAST_LEN)
        xp, xs, w_b = ffn(xp, xs, w_b, 1, 2)
        for acc, val in zip(outs, (pool_p[:, 1:], gla_p, pool_s[:, 1:], gla_s)):
            acc.append(val)

    return (xp.reshape(bp, seq, d), xs.reshape(bs, dec_seq, d),
            jnp.stack(outs[0]), jnp.stack(outs[1]), jnp.stack(outs[2]), jnp.stack(outs[3]))
```
